```python
import jax, jax.numpy as jnp
from jax import lax
import numpy as np

D_MODEL = 1024
BATCH = 8
SEQ = 2048
DEPTH = 2
DEC_BATCH = 128
DEC_SEQ = 8
PAST_LEN = 16384
PAGE_SIZE = 128

N_MIXERS = 2
HGRN_EXPAND = 128
HGRN_HEADS = D_MODEL // HGRN_EXPAND
HGRN_DK = HGRN_EXPAND
HGRN_DV = D_MODEL // HGRN_HEADS
HGRN_FDIM = HGRN_HEADS * HGRN_DK
HGRN_VDIM = HGRN_HEADS * HGRN_DV
HGRN_CHUNK = 32
CONV_W = 3
D_FF = 2816
FFN_RES = 0.5
NORM_EPS = 1e-6
N_HGRN = (DEPTH + 1) // 2
N_CONV = DEPTH // 2

kernel_name = "hgrn2_shortconv_macaron_decode_step"


def rms_norm(x, gain):
    xf = x.astype(jnp.float32)
    r = lax.rsqrt(jnp.mean(xf * xf, axis=-1, keepdims=True) + NORM_EPS)
    return (xf * r).astype(x.dtype) * gain


def swiglu_ffn(x, w_up, w_down):
    gate, up = jnp.split(x @ w_up, 2, axis=-1)
    return (jax.nn.silu(gate) * up) @ w_down


def hgrn2_recurrence(q, k, v, logf, s0):
    B, T, H, _ = q.shape
    chunk = min(HGRN_CHUNK, T)
    n_chunks = -(-T // chunk)
    pad = n_chunks * chunk - T

    def to_chunks(a):
        a = jnp.pad(a, ((0, 0), (0, pad), (0, 0), (0, 0)))
        return jnp.moveaxis(a.reshape(B, n_chunks, chunk, H, a.shape[-1]), 1, 0)

    causal = jnp.tril(jnp.ones((chunk, chunk), dtype=bool))[None, :, :, None, None]

    def step(S, blk):
        qc, kc, vc, gc = blk
        b = jnp.cumsum(gc, axis=1)
        inter = jnp.einsum('bthk,bhkv->bthv', qc * jnp.exp(b), S)
        decay = jnp.exp(jnp.where(causal, b[:, :, None] - b[:, None, :], -jnp.inf))
        scores = jnp.sum(qc[:, :, None] * kc[:, None, :] * decay, axis=-1)
        intra = jnp.einsum('btsh,bshv->bthv', scores, vc)
        b_last = b[:, -1]
        S = jnp.exp(b_last)[..., None] * S + jnp.einsum(
            'bshk,bshv->bhkv', kc * jnp.exp(b_last[:, None] - b), vc)
        return S, inter + intra

    s_final, o = lax.scan(step, s0, (to_chunks(q), to_chunks(k), to_chunks(v), to_chunks(logf)))
    o = jnp.moveaxis(o, 0, 1).reshape(B, n_chunks * chunk, H, -1)[:, :T]
    return o, s_final


def hgrn2_mixer(x, s0, w_in, lower_bound, norm_gain, w_out):
    B, T, _ = x.shape
    proj = x @ w_in
    q, f, i, g = jnp.split(proj, [HGRN_FDIM, 2 * HGRN_FDIM, 2 * HGRN_FDIM + HGRN_VDIM], axis=-1)
    q = jax.nn.silu(q.astype(jnp.float32))
    fgate = lower_bound + (1.0 - lower_bound) * jax.nn.sigmoid(f.astype(jnp.float32))
    logf = jnp.log(fgate)
    k = 1.0 - fgate
    o, s_new = hgrn2_recurrence(
        q.reshape(B, T, HGRN_HEADS, HGRN_DK),
        k.reshape(B, T, HGRN_HEADS, HGRN_DK),
        i.astype(jnp.float32).reshape(B, T, HGRN_HEADS, HGRN_DV),
        logf.reshape(B, T, HGRN_HEADS, HGRN_DK),
        s0.astype(jnp.float32))
    o = rms_norm(o, norm_gain.reshape(HGRN_HEADS, HGRN_DV))
    o = o.reshape(B, T, HGRN_VDIM).astype(x.dtype) * jax.nn.sigmoid(g)
    return o @ w_out, s_new.astype(s0.dtype)


def shortconv_mixer(x, buf, w_in, conv_w, w_out):
    T = x.shape[1]
    b_gate, c_gate, v = jnp.split(x @ w_in, 3, axis=-1)
    u = c_gate * v
    ext = jnp.concatenate([buf.astype(u.dtype), u], axis=1)
    conv = sum(conv_w[tap] * ext[:, tap:tap + T] for tap in range(CONV_W))
    y = (b_gate * conv) @ w_out
    return y, ext[:, -(CONV_W - 1):]


def decoder_trunk(x, state_hgrn, state_conv, norm_ffn1, w_ffn1_up, w_ffn1_down, norm_mix,
                  norm_ffn2, w_ffn2_up, w_ffn2_down, w_hgrn_in, lower_bounds, hgrn_norm,
                  w_hgrn_out, w_conv_in, conv_w, w_conv_out, norm_final):
    new_hgrn, new_conv = [], []
    for layer in range(DEPTH):
        x = x + FFN_RES * swiglu_ffn(rms_norm(x, norm_ffn1[layer]), w_ffn1_up[layer], w_ffn1_down[layer])
        xn = rms_norm(x, norm_mix[layer])
        j = layer // N_MIXERS
        if layer % N_MIXERS == 0:
            mix, s = hgrn2_mixer(xn, state_hgrn[j], w_hgrn_in[j], lower_bounds[j], hgrn_norm[j], w_hgrn_out[j])
            new_hgrn.append(s)
        else:
            mix, s = shortconv_mixer(xn, state_conv[j], w_conv_in[j], conv_w[j], w_conv_out[j])
            new_conv.append(s)
        x = x + mix
        x = x + FFN_RES * swiglu_ffn(rms_norm(x, norm_ffn2[layer]), w_ffn2_up[layer], w_ffn2_down[layer])
    return rms_norm(x, norm_final), jnp.stack(new_hgrn), jnp.stack(new_conv)


def setup_inputs(seed: int = 0) -> dict:
    key = jax.random.key(seed)
    ks = jax.random.split(key, 20)

    def nrm(k, shape, scale):
        return jax.random.normal(k, shape, jnp.float32) * scale

    def gain(k, shape):
        return 1.0 + 0.01 * jax.random.normal(k, shape, jnp.float32)

    return {
        "x_prompt": nrm(ks[0], (BATCH, SEQ, D_MODEL), 1.0),
        "x_sample": nrm(ks[1], (DEC_BATCH, DEC_SEQ, D_MODEL), 1.0),
        "state_hgrn": nrm(ks[2], (N_HGRN, DEC_BATCH, HGRN_HEADS, HGRN_DK, HGRN_DV), 0.5),
        "state_conv": nrm(ks[3], (N_CONV, DEC_BATCH, CONV_W - 1, D_MODEL), 1.0),
        "norm_ffn1": gain(ks[4], (DEPTH, D_MODEL)),
        "w_ffn1_up": nrm(ks[5], (DEPTH, D_MODEL, 2 * D_FF), D_MODEL ** -0.5),
        "w_ffn1_down": nrm(ks[6], (DEPTH, D_FF, D_MODEL), D_FF ** -0.5),
        "norm_mix": gain(ks[7], (DEPTH, D_MODEL)),
        "norm_ffn2": gain(ks[8], (DEPTH, D_MODEL)),
        "w_ffn2_up": nrm(ks[9], (DEPTH, D_MODEL, 2 * D_FF), D_MODEL ** -0.5),
        "w_ffn2_down": nrm(ks[10], (DEPTH, D_FF, D_MODEL), D_FF ** -0.5),
        "w_hgrn_in": nrm(ks[11], (N_HGRN, D_MODEL, 2 * HGRN_FDIM + 2 * HGRN_VDIM), D_MODEL ** -0.5),
        "hgrn_lower_bounds": nrm(ks[12], (N_HGRN + 1, HGRN_FDIM), 0.1),
        "hgrn_norm": gain(ks[13], (N_HGRN, HGRN_VDIM)),
        "w_hgrn_out": nrm(ks[14], (N_HGRN, HGRN_VDIM, D_MODEL), HGRN_VDIM ** -0.5),
        "w_conv_in": nrm(ks[15], (N_CONV, D_MODEL, 3 * D_MODEL), D_MODEL ** -0.5),
        "conv_w": nrm(ks[16], (N_CONV, CONV_W, D_MODEL), 0.5),
        "w_conv_out": nrm(ks[17], (N_CONV, D_MODEL, D_MODEL), D_MODEL ** -0.5),
        "norm_final": gain(ks[18], (D_MODEL,)),
    }


def reference(x_prompt, x_sample, state_hgrn, state_conv, norm_ffn1, w_ffn1_up, w_ffn1_down,
              norm_mix, norm_ffn2, w_ffn2_up, w_ffn2_down, w_hgrn_in, hgrn_lower_bounds,
              hgrn_norm, w_hgrn_out, w_conv_in, conv_w, w_conv_out, norm_final):
    lower_bounds = jnp.cumsum(jax.nn.softmax(hgrn_lower_bounds.astype(jnp.float32), axis=0), axis=0)[:N_HGRN]
    weights = (norm_ffn1, w_ffn1_up, w_ffn1_down, norm_mix, norm_ffn2, w_ffn2_up, w_ffn2_down,
               w_hgrn_in, lower_bounds, hgrn_norm, w_hgrn_out, w_conv_in, conv_w, w_conv_out, norm_final)
    b_prompt = x_prompt.shape[0]
    zero_hgrn = jnp.zeros((N_HGRN, b_prompt, HGRN_HEADS, HGRN_DK, HGRN_DV), x_prompt.dtype)
    zero_conv = jnp.zeros((N_CONV, b_prompt, CONV_W - 1, D_MODEL), x_prompt.dtype)
    y_prompt, state_hgrn_prompt, state_conv_prompt = decoder_trunk(x_prompt, zero_hgrn, zero_conv, *weights)
    y_sample, state_hgrn_sample, state_conv_sample = decoder_trunk(x_sample, state_hgrn, state_conv, *weights)
    return (y_prompt, y_sample, state_hgrn_prompt, state_hgrn_sample, state_conv_prompt, state_conv_sample)
```

```python
import functools

import jax
import jax.numpy as jnp
from jax import lax
from jax.experimental import pallas as pl
from jax.experimental.pallas import tpu as pltpu

F32 = jnp.float32
BF16 = jnp.bfloat16

NORM_EPS = 1e-6
FFN_RES = 0.5
HGRN_HEADS = 8
HGRN_HEAD_DIM = 128
HGRN_MAX_CHUNK = 32
CONV_W = 3
SUBLANES = 8
LANES = 128
FFN_CHUNK = 256
VMEM_LIMIT_BYTES = 58 * 1024 * 1024
MASKED_LOG = -1e30


def _rms(x, gain):
    r = lax.rsqrt(jnp.mean(x * x, axis=-1, keepdims=True) + NORM_EPS)
    return (x * r) * gain


def _dot(a, b):
    return jnp.dot(a, b, preferred_element_type=F32)


def _dot_nt(a, b):
    return lax.dot_general(a, b, (((1,), (1,)), ((), ())), preferred_element_type=F32)


def _dot_tn(a, b):
    return lax.dot_general(a, b, (((0,), (0,)), ((), ())), preferred_element_type=F32)


def _ffn_kernel(x_ref, g_ref, wup_ref, wdn_ref, gf_ref, o_ref, *, n_chunks, final_norm):
    x = x_ref[...]
    xn = _rms(x, g_ref[...]).astype(BF16)
    acc = jnp.zeros_like(x)
    for c in range(n_chunks):
        gate = _dot(xn, wup_ref[0, c])
        up = _dot(xn, wup_ref[1, c])
        act = (gate * jax.nn.sigmoid(gate) * up).astype(BF16)
        acc = acc + _dot(act, wdn_ref[c])
    out = x + FFN_RES * acc
    if final_norm:
        out = _rms(out, gf_ref[...])
    o_ref[...] = out


def _ffn(x, gain, wup, wdn, gain_final, *, final_norm, tm):
    n, d = x.shape
    n_chunks = wdn.shape[0]
    const = lambda nd: (lambda i: (0,) * nd)
    return pl.pallas_call(
        functools.partial(_ffn_kernel, n_chunks=n_chunks, final_norm=final_norm),
        grid=(n // tm,),
        in_specs=[
            pl.BlockSpec((tm, d), lambda i: (i, 0)),
            pl.BlockSpec((1, d), const(2)),
            pl.BlockSpec(wup.shape, const(4)),
            pl.BlockSpec(wdn.shape, const(3)),
            pl.BlockSpec((1, d), const(2)),
        ],
        out_specs=pl.BlockSpec((tm, d), lambda i: (i, 0)),
        out_shape=jax.ShapeDtypeStruct((n, d), F32),
        compiler_params=pltpu.CompilerParams(
            dimension_semantics=("arbitrary",), vmem_limit_bytes=VMEM_LIMIT_BYTES),
        name="ffn_final" if final_norm else "ffn",
    )(x, gain.reshape(1, d), wup, wdn, gain_final.reshape(1, d))


def _hgrn_head_chunk(proj_scr, yg_scr, st_ref, rows, h, lb, hn, ones_bf, chunk):
    dk = HGRN_HEAD_DIM
    vdim = HGRN_HEADS * dk
    nb = chunk // SUBLANES
    lo = h * dk
    qp = proj_scr[rows, lo:lo + dk]
    fp = proj_scr[rows, vdim + lo:vdim + lo + dk]
    v = proj_scr[rows, 2 * vdim + lo:2 * vdim + lo + dk]
    gp = proj_scr[rows, 3 * vdim + lo:3 * vdim + lo + dk]

    q = qp * jax.nn.sigmoid(qp)
    fgate = lb + (1.0 - lb) * jax.nn.sigmoid(fp)
    logf = jnp.log(fgate)
    k = 1.0 - fgate

    shape3 = (nb, SUBLANES, dk)
    q3, k3, v3 = q.reshape(shape3), k.reshape(shape3), v.reshape(shape3)
    ridx = lax.broadcasted_iota(jnp.int32, shape3, 1)

    w3 = logf.reshape(shape3)
    for s in (1, 2, 4):
        w3 = w3 + jnp.where(ridx >= s, pltpu.roll(w3, s, 1), 0.0)
    tot = w3[:, SUBLANES - 1:SUBLANES, :]
    offs = [jnp.zeros((1, 1, dk), F32)]
    for j in range(nb):
        offs.append(offs[-1] + tot[j:j + 1])

    q_blk = q3 * jnp.exp(w3)
    k_blk = k3 * jnp.exp(tot - w3)
    if nb > 1:
        q_in = q_blk * jnp.exp(jnp.concatenate(offs[:nb], axis=0))
        k_end = k_blk * jnp.exp(jnp.concatenate([offs[nb] - offs[j + 1] for j in range(nb)], axis=0))
    else:
        q_in, k_end = q_blk, k_blk

    st = st_ref[...]
    o = _dot_nt(q_in.reshape(chunk, dk).astype(BF16), st.astype(BF16))

    terms = [q3 * k3]
    for d in range(1, SUBLANES):
        valid = ridx >= d
        decay = jnp.exp(jnp.where(valid, w3 - pltpu.roll(w3, d, 1), MASKED_LOG))
        terms.append(q3 * pltpu.roll(k3, d, 1) * decay)
    stacked = jnp.concatenate(terms, axis=0).reshape(SUBLANES * chunk, dk).astype(BF16)
    sums = _dot(stacked, ones_bf).reshape(SUBLANES, nb, SUBLANES, dk)
    o3 = sums[0] * v3
    for d in range(1, SUBLANES):
        o3 = o3 + sums[d] * pltpu.roll(v3, d, 1)
    o = o + o3.reshape(chunk, dk)

    if nb > 1:
        k_rows, v_rows, starts = [], [], []
        n_rows = 0
        for i in range(1, nb):
            starts.append(n_rows)
            for j in range(i):
                k_rows.append(k_blk[j] * jnp.exp(offs[i][0] - offs[j + 1][0]))
                v_rows.append(v3[j])
                n_rows += SUBLANES
        pad = -n_rows % LANES
        if pad:
            k_rows.append(jnp.zeros((pad, dk), F32))
            v_rows.append(jnp.zeros((pad, dk), F32))
        k_st = jnp.concatenate(k_rows, axis=0).astype(BF16)
        v_st = jnp.concatenate(v_rows, axis=0).astype(BF16)
        r_tot = n_rows + pad
        p = _dot_nt(q_blk.reshape(chunk, dk).astype(BF16), k_st)
        row_blk = lax.broadcasted_iota(jnp.int32, (chunk, r_tot), 0) // SUBLANES
        col = lax.broadcasted_iota(jnp.int32, (chunk, r_tot), 1)
        col_blk = jnp.where(col < n_rows, 1, 0)
        for s0 in starts[1:]:
            col_blk = col_blk + jnp.where(col >= s0, 1, 0)
        p = jnp.where(row_blk == col_blk, p, 0.0).astype(BF16)
        o = o + _dot(p, v_st)

    ms = jnp.mean(o * o, axis=-1, keepdims=True)
    on = (o * lax.rsqrt(ms + NORM_EPS)) * hn
    yg_scr[rows, lo:lo + dk] = on * jax.nn.sigmoid(gp)

    upd = _dot_tn(v.astype(BF16), k_end.reshape(chunk, dk).astype(BF16))
    st_ref[...] = jnp.exp(offs[nb][0]) * st + upd


def _hgrn_kernel(x_ref, s0_ref, gmix_ref, win_ref, lbp_ref, hn_ref, wout_ref, y_ref, sout_ref,
                 proj_scr, st_scr, yg_scr, *, bb, tb, chunk, layer_slot):
    t = pl.program_id(1)
    d = x_ref.shape[-1]
    m = bb * tb
    dk = HGRN_HEAD_DIM

    @pl.when(t == 0)
    def _():
        def load(bi, carry):
            for h in range(HGRN_HEADS):
                st_scr[bi, h] = s0_ref[bi, h].T
            return carry
        lax.fori_loop(0, bb, load, 0)

    x = x_ref[...].reshape(m, d)
    xn = _rms(x, gmix_ref[...]).astype(BF16)
    n_cols = win_ref.shape[1]
    col_step = 512
    for j in range(n_cols // col_step):
        proj_scr[:, j * col_step:(j + 1) * col_step] = _dot(xn, win_ref[:, j * col_step:(j + 1) * col_step])

    lbp = lbp_ref[...]
    e = jnp.exp(lbp - jnp.max(lbp, axis=0, keepdims=True))
    sm = e / jnp.sum(e, axis=0, keepdims=True)
    lb_row = sm[0:1]
    for s in range(1, layer_slot + 1):
        lb_row = lb_row + sm[s:s + 1]
    hn_row = hn_ref[...]
    ones_bf = jnp.ones((dk, dk), BF16)

    chunks_per_b = tb // chunk

    def body(ci, carry):
        bi = ci // chunks_per_b if chunks_per_b > 1 else ci
        rows = pl.ds(pl.multiple_of(ci * chunk, chunk), chunk)
        for h in range(HGRN_HEADS):
            _hgrn_head_chunk(proj_scr, yg_scr, st_scr.at[bi, h], rows, h,
                             lb_row[:, h * dk:(h + 1) * dk], hn_row[:, h * dk:(h + 1) * dk],
                             ones_bf, chunk)
        return carry

    lax.fori_loop(0, m // chunk, body, 0)

    y = _dot(yg_scr[...].astype(BF16), wout_ref[...])
    y_ref[...] = (x + y).reshape(bb, tb, d)

    @pl.when(t == pl.num_programs(1) - 1)
    def _():
        def store(bi, carry):
            for h in range(HGRN_HEADS):
                sout_ref[bi, h] = st_scr[bi, h].T
            return carry
        lax.fori_loop(0, bb, store, 0)


def _hgrn(x, s0, gmix, win, lb_params, hnorm, wout, *, layer_slot, bb, tb):
    b, t, d = x.shape
    chunk = min(HGRN_MAX_CHUNK, t)
    assert t % tb == 0 and tb % chunk == 0 and b % bb == 0 and chunk % SUBLANES == 0
    heads, dk = HGRN_HEADS, HGRN_HEAD_DIM
    m = bb * tb
    const = lambda nd: (lambda i, j: (0,) * nd)
    return pl.pallas_call(
        functools.partial(_hgrn_kernel, bb=bb, tb=tb, chunk=chunk, layer_slot=layer_slot),
        grid=(b // bb, t // tb),
        in_specs=[
            pl.BlockSpec((bb, tb, d), lambda i, j: (i, j, 0)),
            pl.BlockSpec((bb, heads, dk, dk), lambda i, j: (i, 0, 0, 0)),
            pl.BlockSpec((1, d), const(2)),
            pl.BlockSpec(win.shape, const(2)),
            pl.BlockSpec(lb_params.shape, const(2)),
            pl.BlockSpec((1, heads * dk), const(2)),
            pl.BlockSpec(wout.shape, const(2)),
        ],
        out_specs=[
            pl.BlockSpec((bb, tb, d), lambda i, j: (i, j, 0)),
            pl.BlockSpec((bb, heads, dk, dk), lambda i, j: (i, 0, 0, 0)),
        ],
        out_shape=[
            jax.ShapeDtypeStruct((b, t, d), F32),
            jax.ShapeDtypeStruct((b, heads, dk, dk), F32),
        ],
        scratch_shapes=[
            pltpu.VMEM((m, win.shape[1]), F32),
            pltpu.VMEM((bb, heads, dk, dk), F32),
            pltpu.VMEM((m, heads * dk), F32),
        ],
        compiler_params=pltpu.CompilerParams(
            dimension_semantics=("arbitrary", "arbitrary"), vmem_limit_bytes=VMEM_LIMIT_BYTES),
        name="hgrn_mixer",
    )(x, s0, gmix.reshape(1, d), win, lb_params, hnorm.reshape(1, heads * dk), wout)


def _conv_kernel(x_ref, buf_ref, gmix_ref, win_ref, cw_ref, wout_ref, y_ref, bufo_ref, ext_scr, *, bb, tb):
    t = pl.program_id(1)
    d = x_ref.shape[-1]
    m = bb * tb
    halo = SUBLANES
    tail = CONV_W - 1

    @pl.when(t == 0)
    def _():
        ext_scr[:, halo - tail:halo, :] = buf_ref[...]

    x = x_ref[...].reshape(m, d)
    xn = _rms(x, gmix_ref[...]).astype(BF16)
    b_gate = _dot(xn, win_ref[:, 0:d])
    c_gate = _dot(xn, win_ref[:, d:2 * d])
    v = _dot(xn, win_ref[:, 2 * d:3 * d])
    ext_scr[:, halo:halo + tb, :] = (c_gate * v).reshape(bb, tb, d)

    cw = cw_ref[...]
    conv = None
    for tap in range(CONV_W):
        start = halo - tail + tap
        term = cw[tap:tap + 1] * ext_scr[:, start:start + tb, :]
        conv = term if conv is None else conv + term
    y = _dot((b_gate * conv.reshape(m, d)).astype(BF16), wout_ref[...])
    y_ref[...] = (x + y).reshape(bb, tb, d)

    new_tail = ext_scr[:, halo + tb - tail:halo + tb, :]
    ext_scr[:, halo - tail:halo, :] = new_tail

    @pl.when(t == pl.num_programs(1) - 1)
    def _():
        bufo_ref[...] = new_tail


def _conv(x, buf, gmix, win, cw, wout, *, bb, tb):
    b, t, d = x.shape
    tail = CONV_W - 1
    assert t % tb == 0 and b % bb == 0 and tb >= tail
    const = lambda nd: (lambda i, j: (0,) * nd)
    return pl.pallas_call(
        functools.partial(_conv_kernel, bb=bb, tb=tb),
        grid=(b // bb, t // tb),
        in_specs=[
            pl.BlockSpec((bb, tb, d), lambda i, j: (i, j, 0)),
            pl.BlockSpec((bb, tail, d), lambda i, j: (i, 0, 0)),
            pl.BlockSpec((1, d), const(2)),
            pl.BlockSpec(win.shape, const(2)),
            pl.BlockSpec(cw.shape, const(2)),
            pl.BlockSpec(wout.shape, const(2)),
        ],
        out_specs=[
            pl.BlockSpec((bb, tb, d), lambda i, j: (i, j, 0)),
            pl.BlockSpec((bb, tail, d), lambda i, j: (i, 0, 0)),
        ],
        out_shape=[
            jax.ShapeDtypeStruct((b, t, d), F32),
            jax.ShapeDtypeStruct((b, tail, d), F32),
        ],
        scratch_shapes=[pltpu.VMEM((bb, tb + SUBLANES, d), F32)],
        compiler_params=pltpu.CompilerParams(
            dimension_semantics=("arbitrary", "arbitrary"), vmem_limit_bytes=VMEM_LIMIT_BYTES),
        name="conv_mixer",
    )(x, buf, gmix.reshape(1, d), win, cw, wout)


def _prep_ffn_weights(w_up, w_down):
    d, two_ff = w_up.shape
    ff = two_ff // 2
    n_chunks = ff // FFN_CHUNK
    wup = w_up.astype(BF16).reshape(d, 2, n_chunks, FFN_CHUNK).transpose(1, 2, 0, 3)
    wdn = w_down.astype(BF16).reshape(n_chunks, FFN_CHUNK, d)
    return wup, wdn


def _trunk(x, state_hgrn, state_conv, p, *, ffn_tm, hgrn_tiles, conv_tiles):
    b, t, d = x.shape
    depth = p["norm_ffn1"].shape[0]
    new_hgrn, new_conv = [], []
    xf = x.reshape(b * t, d)
    for layer in range(depth):
        xf = _ffn(xf, p["norm_ffn1"][layer], *p["ffn1"][layer], p["norm_final"], final_norm=False, tm=ffn_tm)
        j = layer // 2
        if layer % 2 == 0:
            x3, s = _hgrn(xf.reshape(b, t, d), state_hgrn[j], p["norm_mix"][layer], p["w_hgrn_in"][j],
                          p["hgrn_lower_bounds"], p["hgrn_norm"][j], p["w_hgrn_out"][j],
                          layer_slot=j, bb=hgrn_tiles[0], tb=hgrn_tiles[1])
            new_hgrn.append(s)
        else:
            x3, s = _conv(xf.reshape(b, t, d), state_conv[j], p["norm_mix"][layer], p["w_conv_in"][j],
                          p["conv_w"][j], p["w_conv_out"][j], bb=conv_tiles[0], tb=conv_tiles[1])
            new_conv.append(s)
        xf = _ffn(x3.reshape(b * t, d), p["norm_ffn2"][layer], *p["ffn2"][layer], p["norm_final"],
                  final_norm=(layer == depth - 1), tm=ffn_tm)
    return xf.reshape(b, t, d), jnp.stack(new_hgrn), jnp.stack(new_conv)


def kernel(x_prompt, x_sample, state_hgrn, state_conv, norm_ffn1, w_ffn1_up, w_ffn1_down, norm_mix, norm_ffn2, w_ffn2_up, w_ffn2_down, w_hgrn_in, hgrn_lower_bounds, hgrn_norm, w_hgrn_out, w_conv_in, conv_w, w_conv_out, norm_final):
    depth = norm_ffn1.shape[0]
    p = {
        "norm_ffn1": norm_ffn1, "norm_mix": norm_mix, "norm_ffn2": norm_ffn2, "norm_final": norm_final,
        "ffn1": [_prep_ffn_weights(w_ffn1_up[l], w_ffn1_down[l]) for l in range(depth)],
        "ffn2": [_prep_ffn_weights(w_ffn2_up[l], w_ffn2_down[l]) for l in range(depth)],
        "w_hgrn_in": w_hgrn_in.astype(BF16), "w_hgrn_out": w_hgrn_out.astype(BF16),
        "hgrn_lower_bounds": hgrn_lower_bounds.astype(F32), "hgrn_norm": hgrn_norm,
        "w_conv_in": w_conv_in.astype(BF16), "conv_w": conv_w, "w_conv_out": w_conv_out.astype(BF16),
    }
    bp = x_prompt.shape[0]
    n_hgrn, _, heads, dk, dv = state_hgrn.shape
    n_conv, _, tail, d = state_conv.shape
    zero_hgrn = jnp.zeros((n_hgrn, bp, heads, dk, dv), x_prompt.dtype)
    zero_conv = jnp.zeros((n_conv, bp, tail, d), x_prompt.dtype)

    y_p, sh_p, sc_p = _trunk(x_prompt, zero_hgrn, zero_conv, p,
                             ffn_tm=512, hgrn_tiles=(1, 256), conv_tiles=(1, 512))
    y_s, sh_s, sc_s = _trunk(x_sample, state_hgrn, state_conv, p,
                             ffn_tm=512, hgrn_tiles=(8, x_sample.shape[1]), conv_tiles=(16, x_sample.shape[1]))
    return (y_p, y_s, sh_p, sh_s, sc_p, sc_s)
```

```python
import functools

import jax
import jax.numpy as jnp
from jax import lax
from jax.experimental import pallas as pl
from jax.experimental.pallas import tpu as pltpu

F32 = jnp.float32
BF16 = jnp.bfloat16

NORM_EPS = 1e-6
FFN_RES = 0.5
HGRN_HEADS = 8
HGRN_HEAD_DIM = 128
HGRN_MAX_CHUNK = 32
CONV_W = 3
SUBLANES = 8
BF16_ROWS = 16
LANES = 128
FFN_CHUNK = 256
PROJ_COLS = 512
VMEM_LIMIT_BYTES = 58 * 1024 * 1024


def _rms(x, gain):
    r = lax.rsqrt(jnp.mean(x * x, axis=-1, keepdims=True) + NORM_EPS)
    return (x * r) * gain


def _dot(a, b):
    return jnp.dot(a, b, preferred_element_type=F32)


def _dot_nt(a, b):
    return lax.dot_general(a, b, (((1,), (1,)), ((), ())), preferred_element_type=F32)


def _dot_tn(a, b):
    return lax.dot_general(a, b, (((0,), (0,)), ((), ())), preferred_element_type=F32)


def _ffn_kernel(x_ref, g_ref, wup_ref, wdn_ref, gf_ref, o_ref, *, final_norm):
    x = x_ref[...]
    xn = _rms(x, g_ref[...]).astype(BF16)
    ff = wdn_ref.shape[0]
    acc = jnp.zeros_like(x)
    for c0 in range(0, ff, FFN_CHUNK):
        gate = _dot(xn, wup_ref[:, c0:c0 + FFN_CHUNK])
        up = _dot(xn, wup_ref[:, ff + c0:ff + c0 + FFN_CHUNK])
        act = (gate * jax.nn.sigmoid(gate) * up).astype(BF16)
        acc = acc + _dot(act, wdn_ref[c0:c0 + FFN_CHUNK, :])
    out = x + FFN_RES * acc
    if final_norm:
        out = _rms(out, gf_ref[...])
    o_ref[...] = out


def _ffn(x, gain, wup, wdn, gain_final, *, final_norm, tm):
    n, d = x.shape
    assert n % tm == 0 and wdn.shape[0] % FFN_CHUNK == 0 and wup.shape[1] == 2 * wdn.shape[0]
    const = lambda i: (0, 0)
    return pl.pallas_call(
        functools.partial(_ffn_kernel, final_norm=final_norm),
        grid=(n // tm,),
        in_specs=[
            pl.BlockSpec((tm, d), lambda i: (i, 0)),
            pl.BlockSpec((1, d), const),
            pl.BlockSpec(wup.shape, const),
            pl.BlockSpec(wdn.shape, const),
            pl.BlockSpec((1, d), const),
        ],
        out_specs=pl.BlockSpec((tm, d), lambda i: (i, 0)),
        out_shape=jax.ShapeDtypeStruct((n, d), F32),
        compiler_params=pltpu.CompilerParams(
            dimension_semantics=("arbitrary",), vmem_limit_bytes=VMEM_LIMIT_BYTES),
        name="ffn_final" if final_norm else "ffn",
    )(x, gain.reshape(1, d), wup, wdn, gain_final.reshape(1, d))


def _hgrn_head_chunk(act_scr, yg_scr, st_in, st_out, rows, h, hn, ones_bf, ones_rows_bf, chunk):
    dk = HGRN_HEAD_DIM
    vdim = HGRN_HEADS * dk
    nb = chunk // SUBLANES
    lo = h * dk
    q = act_scr[rows, lo:lo + dk]
    fg = act_scr[rows, vdim + lo:vdim + lo + dk]
    lf = act_scr[rows, 2 * vdim + lo:2 * vdim + lo + dk]
    v = act_scr[rows, 3 * vdim + lo:3 * vdim + lo + dk]
    gs = act_scr[rows, 4 * vdim + lo:4 * vdim + lo + dk]
    k = 1.0 - fg

    shape3 = (nb, SUBLANES, dk)
    q3, k3, v3, fg3 = q.reshape(shape3), k.reshape(shape3), v.reshape(shape3), fg.reshape(shape3)
    ridx = lax.broadcasted_iota(jnp.int32, shape3, 1)

    w3 = lf.reshape(shape3)
    for s in (1, 2, 4):
        w3 = w3 + jnp.where(ridx >= s, pltpu.roll(w3, s, 1), 0.0)
    tot = w3[:, SUBLANES - 1:SUBLANES, :]
    offs = [jnp.zeros((1, 1, dk), F32)]
    for j in range(nb):
        offs.append(offs[-1] + tot[j:j + 1])

    q_blk = q3 * jnp.exp2(w3)
    k_blk = k3 * jnp.exp2(tot - w3)
    if nb > 1:
        q_in = q_blk * jnp.exp2(jnp.concatenate(offs[:nb], axis=0))
        k_end = k_blk * jnp.exp2(jnp.concatenate([offs[nb] - offs[j + 1] for j in range(nb)], axis=0))
    else:
        q_in, k_end = q_blk, k_blk

    st = st_in[...]
    o = _dot(q_in.reshape(chunk, dk).astype(BF16), st.astype(BF16))

    fm = jnp.where(ridx == 0, 0.0, fg3)
    decay = fm
    terms = [q3 * k3]
    for d in range(1, SUBLANES):
        if d > 1:
            decay = decay * pltpu.roll(fm, d - 1, 1)
        terms.append((q3 * decay) * pltpu.roll(k3, d, 1))
    stacked = jnp.concatenate(terms, axis=0).reshape(SUBLANES * chunk, dk).astype(BF16)
    sums = _dot(stacked, ones_bf).reshape(SUBLANES, nb, SUBLANES, dk)
    o3 = sums[0] * v3
    for d in range(1, SUBLANES):
        o3 = o3 + sums[d] * pltpu.roll(v3, d, 1)
    o = o + o3.reshape(chunk, dk)

    if nb > 1:
        k_rows, v_rows, starts = [], [], []
        n_rows = 0
        for i in range(1, nb):
            starts.append(n_rows)
            for j in range(i):
                k_rows.append(k_blk[j] * jnp.exp2(offs[i][0] - offs[j + 1][0]))
                v_rows.append(v3[j])
                n_rows += SUBLANES
        pad = -n_rows % LANES
        if pad:
            k_rows.append(jnp.zeros((pad, dk), F32))
            v_rows.append(jnp.zeros((pad, dk), F32))
        k_st = jnp.concatenate(k_rows, axis=0).astype(BF16)
        v_st = jnp.concatenate(v_rows, axis=0).astype(BF16)
        r_tot = n_rows + pad
        p = _dot_nt(q_blk.reshape(chunk, dk).astype(BF16), k_st)
        row_blk = lax.broadcasted_iota(jnp.int32, (chunk, r_tot), 0) // SUBLANES
        col = lax.broadcasted_iota(jnp.int32, (chunk, r_tot), 1)
        col_blk = jnp.where(col < n_rows, 1, 0)
        for s0 in starts[1:]:
            col_blk = col_blk + jnp.where(col >= s0, 1, 0)
        p = jnp.where(row_blk == col_blk, p, 0.0).astype(BF16)
        o = o + _dot(p, v_st)

    ms = jnp.mean(o * o, axis=-1, keepdims=True)
    yg_scr[rows, lo:lo + dk] = ((o * lax.rsqrt(ms + NORM_EPS)) * hn) * gs

    e_tot = jnp.exp2(offs[nb][0])
    e_hi = e_tot.astype(BF16).astype(F32)
    e_mid = (e_tot - e_hi).astype(BF16).astype(F32)
    e_lo = e_tot - e_hi - e_mid
    prow = lax.broadcasted_iota(jnp.int32, (BF16_ROWS, dk), 0)
    pieces = jnp.where(prow == 0, e_hi, jnp.where(prow == 1, e_mid, jnp.where(prow == 2, e_lo, 0.0)))
    lhs = jnp.concatenate([k_end.reshape(chunk, dk), pieces], axis=0).astype(BF16)
    v_pad = jnp.concatenate([v, jnp.zeros((BF16_ROWS, dk), F32)], axis=0).astype(BF16)
    res = _dot_tn(lhs, jnp.concatenate([v_pad, ones_rows_bf], axis=1))
    st_out[...] = res[:, dk:] * st + res[:, :dk]


def _hgrn_kernel(x_ref, s0_ref, gmix_ref, win_ref, lbp_ref, hn_ref, wout_ref, y_ref, sout_ref,
                 act_scr, st_scr, yg_scr, *, bb, tb, chunk, layer_slot):
    t = pl.program_id(1)
    d = x_ref.shape[-1]
    m = bb * tb
    dk = HGRN_HEAD_DIM
    vdim = HGRN_HEADS * dk
    single_time_block = st_scr is None

    if not single_time_block:
        @pl.when(t == 0)
        def _():
            st_scr[...] = s0_ref[...]

    lbp = lbp_ref[...]
    e = jnp.exp(lbp - jnp.max(lbp, axis=0, keepdims=True))
    sm = e / jnp.sum(e, axis=0, keepdims=True)
    lb_row = sm[0:1]
    for s in range(1, layer_slot + 1):
        lb_row = lb_row + sm[s:s + 1]

    x = x_ref[...].reshape(m, d)
    xn = _rms(x, gmix_ref[...]).astype(BF16)
    for c0 in range(0, 4 * vdim, PROJ_COLS):
        blk = _dot(xn, win_ref[:, c0:c0 + PROJ_COLS])
        sect, s0 = divmod(c0, vdim)
        if sect == 0:
            act_scr[:, s0:s0 + PROJ_COLS] = blk * jax.nn.sigmoid(blk)
        elif sect == 1:
            lb = lb_row[:, s0:s0 + PROJ_COLS]
            fgate = lb + (1.0 - lb) * jax.nn.sigmoid(blk)
            act_scr[:, vdim + s0:vdim + s0 + PROJ_COLS] = fgate
            act_scr[:, 2 * vdim + s0:2 * vdim + s0 + PROJ_COLS] = jnp.log2(fgate)
        elif sect == 2:
            act_scr[:, 3 * vdim + s0:3 * vdim + s0 + PROJ_COLS] = blk
        else:
            act_scr[:, 4 * vdim + s0:4 * vdim + s0 + PROJ_COLS] = jax.nn.sigmoid(blk)

    hn_row = hn_ref[...]
    ones_bf = jnp.ones((dk, dk), BF16)
    orow = lax.broadcasted_iota(jnp.int32, (chunk + BF16_ROWS, dk), 0)
    ones_rows_bf = jnp.where((orow >= chunk) & (orow < chunk + 3), 1.0, 0.0).astype(BF16)
    chunks_per_b = tb // chunk

    for ci in range(m // chunk):
        bi = ci // chunks_per_b
        rows = slice(ci * chunk, (ci + 1) * chunk)
        for h in range(HGRN_HEADS):
            if single_time_block:
                st_in, st_out = s0_ref.at[bi, h], sout_ref.at[bi, h]
            else:
                st_in = st_out = st_scr.at[bi, h]
            _hgrn_head_chunk(act_scr, yg_scr, st_in, st_out, rows, h, hn_row[:, h * dk:(h + 1) * dk],
                             ones_bf, ones_rows_bf, chunk)

    y = _dot(yg_scr[...].astype(BF16), wout_ref[...])
    y_ref[...] = (x + y).reshape(bb, tb, d)

    if not single_time_block:
        @pl.when(t == pl.num_programs(1) - 1)
        def _():
            sout_ref[...] = st_scr[...]


def _hgrn(x, s0, gmix, win, lb_params, hnorm, wout, *, layer_slot, bb, tb):
    b, t, d = x.shape
    chunk = min(HGRN_MAX_CHUNK, t)
    assert t % tb == 0 and tb % chunk == 0 and b % bb == 0 and chunk % SUBLANES == 0
    heads, dk = HGRN_HEADS, HGRN_HEAD_DIM
    vdim = heads * dk
    assert win.shape[1] == 4 * vdim and vdim % PROJ_COLS == 0
    m = bb * tb
    single_time_block = t == tb
    const = lambda i, j: (0, 0)
    kern = functools.partial(_hgrn_kernel, bb=bb, tb=tb, chunk=chunk, layer_slot=layer_slot)
    if single_time_block:
        scratch = [pltpu.VMEM((m, 5 * vdim), F32), pltpu.VMEM((m, vdim), F32)]
        body = lambda *refs: kern(*refs[:10], None, refs[10])
    else:
        scratch = [pltpu.VMEM((m, 5 * vdim), F32), pltpu.VMEM((bb, heads, dk, dk), F32),
                   pltpu.VMEM((m, vdim), F32)]
        body = kern
    return pl.pallas_call(
        body,
        grid=(b // bb, t // tb),
        in_specs=[
            pl.BlockSpec((bb, tb, d), lambda i, j: (i, j, 0)),
            pl.BlockSpec((bb, heads, dk, dk), lambda i, j: (i, 0, 0, 0)),
            pl.BlockSpec((1, d), const),
            pl.BlockSpec(win.shape, const),
            pl.BlockSpec(lb_params.shape, const),
            pl.BlockSpec((1, vdim), const),
            pl.BlockSpec(wout.shape, const),
        ],
        out_specs=[
            pl.BlockSpec((bb, tb, d), lambda i, j: (i, j, 0)),
            pl.BlockSpec((bb, heads, dk, dk), lambda i, j: (i, 0, 0, 0)),
        ],
        out_shape=[
            jax.ShapeDtypeStruct((b, t, d), F32),
            jax.ShapeDtypeStruct((b, heads, dk, dk), F32),
        ],
        scratch_shapes=scratch,
        compiler_params=pltpu.CompilerParams(
            dimension_semantics=("arbitrary", "arbitrary"), vmem_limit_bytes=VMEM_LIMIT_BYTES),
        name="hgrn_mixer",
    )(x, s0, gmix.reshape(1, d), win, lb_params, hnorm.reshape(1, vdim), wout)


def _conv_kernel(x_ref, buf_ref, gmix_ref, win_ref, cw_ref, wout_ref, y_ref, bufo_ref, ext_scr, *, bb, tb):
    t = pl.program_id(1)
    d = x_ref.shape[-1]
    m = bb * tb
    halo = SUBLANES
    tail = CONV_W - 1

    @pl.when(t == 0)
    def _():
        ext_scr[:, halo - tail:halo, :] = buf_ref[...]

    x = x_ref[...].reshape(m, d)
    xn = _rms(x, gmix_ref[...]).astype(BF16)
    b_gate = _dot(xn, win_ref[:, 0:d])
    c_gate = _dot(xn, win_ref[:, d:2 * d])
    v = _dot(xn, win_ref[:, 2 * d:3 * d])
    ext_scr[:, halo:halo + tb, :] = (c_gate * v).reshape(bb, tb, d)

    cw = cw_ref[...]
    conv = None
    for tap in range(CONV_W):
        start = halo - tail + tap
        term = cw[tap:tap + 1] * ext_scr[:, start:start + tb, :]
        conv = term if conv is None else conv + term
    y = _dot((b_gate * conv.reshape(m, d)).astype(BF16), wout_ref[...])
    y_ref[...] = (x + y).reshape(bb, tb, d)

    new_tail = ext_scr[:, halo + tb - tail:halo + tb, :]
    ext_scr[:, halo - tail:halo, :] = new_tail

    @pl.when(t == pl.num_programs(1) - 1)
    def _():
        bufo_ref[...] = new_tail


def _conv(x, buf, gmix, win, cw, wout, *, bb, tb):
    b, t, d = x.shape
    tail = CONV_W - 1
    assert t % tb == 0 and b % bb == 0 and tb >= tail
    const = lambda i, j: (0, 0)
    return pl.pallas_call(
        functools.partial(_conv_kernel, bb=bb, tb=tb),
        grid=(b // bb, t // tb),
        in_specs=[
            pl.BlockSpec((bb, tb, d), lambda i, j: (i, j, 0)),
            pl.BlockSpec((bb, tail, d), lambda i, j: (i, 0, 0)),
            pl.BlockSpec((1, d), const),
            pl.BlockSpec(win.shape, const),
            pl.BlockSpec(cw.shape, const),
            pl.BlockSpec(wout.shape, const),
        ],
        out_specs=[
            pl.BlockSpec((bb, tb, d), lambda i, j: (i, j, 0)),
            pl.BlockSpec((bb, tail, d), lambda i, j: (i, 0, 0)),
        ],
        out_shape=[
            jax.ShapeDtypeStruct((b, t, d), F32),
            jax.ShapeDtypeStruct((b, tail, d), F32),
        ],
        scratch_shapes=[pltpu.VMEM((bb, tb + SUBLANES, d), F32)],
        compiler_params=pltpu.CompilerParams(
            dimension_semantics=("arbitrary", "arbitrary"), vmem_limit_bytes=VMEM_LIMIT_BYTES),
        name="conv_mixer",
    )(x, buf, gmix.reshape(1, d), win, cw, wout)


def _trunk(x, state_hgrn, state_conv, p, *, ffn_tm, hgrn_tiles, conv_tiles):
    b, t, d = x.shape
    depth = p["norm_ffn1"].shape[0]
    new_hgrn, new_conv = [], []
    xf = x.reshape(b * t, d)
    for layer in range(depth):
        xf = _ffn(xf, p["norm_ffn1"][layer], p["w_ffn1_up"][layer], p["w_ffn1_down"][layer],
                  p["norm_final"], final_norm=False, tm=ffn_tm)
        j = layer // 2
        if layer % 2 == 0:
            x3, s = _hgrn(xf.reshape(b, t, d), state_hgrn[j], p["norm_mix"][layer], p["w_hgrn_in"][j],
                          p["hgrn_lower_bounds"], p["hgrn_norm"][j], p["w_hgrn_out"][j],
                          layer_slot=j, bb=hgrn_tiles[0], tb=hgrn_tiles[1])
            new_hgrn.append(s)
        else:
            x3, s = _conv(xf.reshape(b, t, d), state_conv[j], p["norm_mix"][layer], p["w_conv_in"][j],
                          p["conv_w"][j], p["w_conv_out"][j], bb=conv_tiles[0], tb=conv_tiles[1])
            new_conv.append(s)
        xf = _ffn(x3.reshape(b * t, d), p["norm_ffn2"][layer], p["w_ffn2_up"][layer], p["w_ffn2_down"][layer],
                  p["norm_final"], final_norm=(layer == depth - 1), tm=ffn_tm)
    return xf.reshape(b, t, d), jnp.stack(new_hgrn), jnp.stack(new_conv)


def kernel(x_prompt, x_sample, state_hgrn, state_conv, norm_ffn1, w_ffn1_up, w_ffn1_down, norm_mix, norm_ffn2, w_ffn2_up, w_ffn2_down, w_hgrn_in, hgrn_lower_bounds, hgrn_norm, w_hgrn_out, w_conv_in, conv_w, w_conv_out, norm_final):
    p = {
        "norm_ffn1": norm_ffn1, "norm_mix": norm_mix, "norm_ffn2": norm_ffn2, "norm_final": norm_final,
        "w_ffn1_up": w_ffn1_up.astype(BF16), "w_ffn1_down": w_ffn1_down.astype(BF16),
        "w_ffn2_up": w_ffn2_up.astype(BF16), "w_ffn2_down": w_ffn2_down.astype(BF16),
        "w_hgrn_in": w_hgrn_in.astype(BF16), "w_hgrn_out": w_hgrn_out.astype(BF16),
        "hgrn_lower_bounds": hgrn_lower_bounds.astype(F32), "hgrn_norm": hgrn_norm,
        "w_conv_in": w_conv_in.astype(BF16), "conv_w": conv_w, "w_conv_out": w_conv_out.astype(BF16),
    }
    bp = x_prompt.shape[0]
    n_hgrn, _, heads, dk, dv = state_hgrn.shape
    n_conv, _, tail, d = state_conv.shape
    zero_hgrn = jnp.zeros((n_hgrn, bp, heads, dk, dv), x_prompt.dtype)
    zero_conv = jnp.zeros((n_conv, bp, tail, d), x_prompt.dtype)

    y_p, sh_p, sc_p = _trunk(x_prompt, zero_hgrn, zero_conv, p,
                             ffn_tm=512, hgrn_tiles=(1, 256), conv_tiles=(1, 512))
    y_s, sh_s, sc_s = _trunk(x_sample, state_hgrn, state_conv, p,
                             ffn_tm=512, hgrn_tiles=(8, x_sample.shape[1]), conv_tiles=(16, x_sample.shape[1]))
    return (y_p, y_s, sh_p, sh_s, sc_p, sc_s)
```

```python
import functools

import jax
import jax.numpy as jnp
from jax import lax
from jax.experimental import pallas as pl
from jax.experimental.pallas import tpu as pltpu

F32 = jnp.float32
BF16 = jnp.bfloat16

NORM_EPS = 1e-6
FFN_RES = 0.5
HGRN_HEADS = 8
HGRN_HEAD_DIM = 128
HGRN_MAX_CHUNK = 32
CONV_W = 3
SUBLANES = 8
BF16_ROWS = 16
LANES = 128
FFN_CHUNK = 256
PROJ_COLS = 256
VMEM_LIMIT_BYTES = 58 * 1024 * 1024


def _rms(x, gain):
    r = lax.rsqrt(jnp.mean(x * x, axis=-1, keepdims=True) + NORM_EPS)
    return (x * r) * gain


def _dot(a, b):
    return jnp.dot(a, b, preferred_element_type=F32)


def _dot_nt(a, b):
    return lax.dot_general(a, b, (((1,), (1,)), ((), ())), preferred_element_type=F32)


def _dot_tn(a, b):
    return lax.dot_general(a, b, (((0,), (0,)), ((), ())), preferred_element_type=F32)


def _layer_spec(w, layer, n_grid):
    zeros = (0,) * (w.ndim - 1)
    index_map = (lambda i: (layer,) + zeros) if n_grid == 1 else (lambda i, j: (layer,) + zeros)
    return pl.BlockSpec((None,) + w.shape[1:], index_map)


def _rows(p):
    return p.reshape(p.shape[0], 1, p.shape[1])


def _ffn_kernel(x_ref, g_ref, wup_ref, wdn_ref, gf_ref, o_ref, *, final_norm):
    x = x_ref[...]
    xn = _rms(x, g_ref[...]).astype(BF16)
    ff = wdn_ref.shape[0]
    acc = jnp.zeros_like(x)
    for c0 in range(0, ff, FFN_CHUNK):
        gate = _dot(xn, wup_ref[:, c0:c0 + FFN_CHUNK])
        up = _dot(xn, wup_ref[:, ff + c0:ff + c0 + FFN_CHUNK])
        act = (gate * jax.nn.sigmoid(gate) * up).astype(BF16)
        acc = acc + _dot(act, wdn_ref[c0:c0 + FFN_CHUNK, :])
    out = x + FFN_RES * acc
    if final_norm:
        out = _rms(out, gf_ref[...])
    o_ref[...] = out


def _ffn(x, gains, wup, wdn, gain_final, layer, *, final_norm, tm):
    n, d = x.shape
    ff = wdn.shape[1]
    assert n % tm == 0 and ff % FFN_CHUNK == 0 and wup.shape[2] == 2 * ff
    return pl.pallas_call(
        functools.partial(_ffn_kernel, final_norm=final_norm),
        grid=(n // tm,),
        in_specs=[
            pl.BlockSpec((tm, d), lambda i: (i, 0)),
            _layer_spec(gains, layer, 1),
            _layer_spec(wup, layer, 1),
            _layer_spec(wdn, layer, 1),
            pl.BlockSpec((1, d), lambda i: (0, 0)),
        ],
        out_specs=pl.BlockSpec((tm, d), lambda i: (i, 0)),
        out_shape=jax.ShapeDtypeStruct((n, d), F32),
        compiler_params=pltpu.CompilerParams(
            dimension_semantics=("arbitrary",), vmem_limit_bytes=VMEM_LIMIT_BYTES),
        name="ffn_final" if final_norm else "ffn",
    )(x, gains, wup, wdn, gain_final.reshape(1, d))


def _hgrn_near_pairs(act_scr, o_scr, rows, h):
    dk = HGRN_HEAD_DIM
    vdim = HGRN_HEADS * dk
    lo = h * dk
    q = act_scr[rows, lo:lo + dk]
    fg = act_scr[rows, vdim + lo:vdim + lo + dk]
    v = act_scr[rows, 3 * vdim + lo:3 * vdim + lo + dk]
    n_rows = q.shape[0]
    shape3 = (n_rows // SUBLANES, SUBLANES, dk)
    q3, fg3, v3 = q.reshape(shape3), fg.reshape(shape3), v.reshape(shape3)
    k3 = 1.0 - fg3
    ridx = lax.broadcasted_iota(jnp.int32, shape3, 1)
    fm = jnp.where(ridx == 0, 0.0, fg3)
    decay = fm
    o3 = jnp.sum(q3 * k3, axis=-1, keepdims=True) * v3
    for d in range(1, SUBLANES):
        if d > 1:
            decay = decay * pltpu.roll(fm, d - 1, 1)
        score = jnp.sum((q3 * decay) * pltpu.roll(k3, d, 1), axis=-1, keepdims=True)
        o3 = o3 + score * pltpu.roll(v3, d, 1)
    o_scr[rows, lo:lo + dk] = o3.reshape(n_rows, dk)


def _hgrn_operands(act_scr, rows, h, chunk):
    dk = HGRN_HEAD_DIM
    vdim = HGRN_HEADS * dk
    nb = chunk // SUBLANES
    lo = h * dk
    q = act_scr[rows, lo:lo + dk]
    fg = act_scr[rows, vdim + lo:vdim + lo + dk]
    lf = act_scr[rows, 2 * vdim + lo:2 * vdim + lo + dk]
    v = act_scr[rows, 3 * vdim + lo:3 * vdim + lo + dk]
    k = 1.0 - fg

    shape3 = (nb, SUBLANES, dk)
    q3, k3, v3 = q.reshape(shape3), k.reshape(shape3), v.reshape(shape3)
    ridx = lax.broadcasted_iota(jnp.int32, shape3, 1)

    w3 = lf.reshape(shape3)
    for s in (1, 2, 4):
        w3 = w3 + jnp.where(ridx >= s, pltpu.roll(w3, s, 1), 0.0)
    tot = w3[:, SUBLANES - 1:SUBLANES, :]
    offs = [jnp.zeros((1, 1, dk), F32)]
    for j in range(nb):
        offs.append(offs[-1] + tot[j:j + 1])

    q_blk = q3 * jnp.exp2(w3)
    k_blk = k3 * jnp.exp2(tot - w3)
    if nb > 1:
        q_in = q_blk * jnp.exp2(jnp.concatenate(offs[:nb], axis=0))
        k_end = k_blk * jnp.exp2(jnp.concatenate([offs[nb] - offs[j + 1] for j in range(nb)], axis=0))
    else:
        q_in, k_end = q_blk, k_blk

    k_st = v_st = far_mask = None
    if nb > 1:
        k_rows, v_rows, starts = [], [], []
        n_rows = 0
        for i in range(1, nb):
            starts.append(n_rows)
            for j in range(i):
                k_rows.append(k_blk[j] * jnp.exp2(offs[i][0] - offs[j + 1][0]))
                v_rows.append(v3[j])
                n_rows += SUBLANES
        pad = -n_rows % BF16_ROWS
        if pad:
            k_rows.append(jnp.zeros((pad, dk), F32))
            v_rows.append(jnp.zeros((pad, dk), F32))
        k_st = jnp.concatenate(k_rows, axis=0).astype(BF16)
        v_st = jnp.concatenate(v_rows, axis=0).astype(BF16)
        r_tot = n_rows + pad
        row_blk = lax.broadcasted_iota(jnp.int32, (chunk, r_tot), 0) // SUBLANES
        col = lax.broadcasted_iota(jnp.int32, (chunk, r_tot), 1)
        col_blk = jnp.where(col < n_rows, 1, 0)
        for s0 in starts[1:]:
            col_blk = col_blk + jnp.where(col >= s0, 1, 0)
        far_mask = row_blk == col_blk

    e_tot = jnp.exp2(offs[nb][0])
    e_hi = e_tot.astype(BF16).astype(F32)
    e_mid = (e_tot - e_hi).astype(BF16).astype(F32)
    e_lo = e_tot - e_hi - e_mid
    prow = lax.broadcasted_iota(jnp.int32, (BF16_ROWS, dk), 0)
    pieces = jnp.where(prow == 0, e_hi, jnp.where(prow == 1, e_mid, jnp.where(prow == 2, e_lo, 0.0)))
    lhs_state = jnp.concatenate([k_end.reshape(chunk, dk), pieces], axis=0).astype(BF16)
    v_pad = jnp.concatenate([v, jnp.zeros((BF16_ROWS, dk), F32)], axis=0).astype(BF16)
    return (q_in.reshape(chunk, dk).astype(BF16), q_blk.reshape(chunk, dk).astype(BF16),
            k_st, v_st, far_mask, lhs_state, v_pad)


def _hgrn_chunk(act_scr, yg_scr, state_refs, rows, hn_row, ones_rows_bf, chunk):
    dk = HGRN_HEAD_DIM
    heads = range(HGRN_HEADS)
    for h in heads:
        _hgrn_near_pairs(act_scr, yg_scr, rows, h)
    ops = [_hgrn_operands(act_scr, rows, h, chunk) for h in heads]
    st = [state_refs[h][0][...] for h in heads]
    o = [_dot(ops[h][0], st[h].astype(BF16)) + yg_scr[rows, h * dk:(h + 1) * dk] for h in heads]
    if ops[0][2] is not None:
        p = [jnp.where(ops[h][4], _dot_nt(ops[h][1], ops[h][2]), 0.0).astype(BF16) for h in heads]
    for h in heads:
        res = _dot_tn(ops[h][5], jnp.concatenate([ops[h][6], ones_rows_bf], axis=1))
        state_refs[h][1][...] = res[:, dk:] * st[h] + res[:, :dk]
    vdim = HGRN_HEADS * dk
    for h in heads:
        oh = o[h]
        if ops[0][2] is not None:
            oh = oh + _dot(p[h], ops[h][3])
        ms = jnp.mean(oh * oh, axis=-1, keepdims=True)
        gs = act_scr[rows, 4 * vdim + h * dk:4 * vdim + (h + 1) * dk]
        yg_scr[rows, h * dk:(h + 1) * dk] = ((oh * lax.rsqrt(ms + NORM_EPS)) * hn_row[:, h * dk:(h + 1) * dk]) * gs


def _hgrn_normed(x_ref, gmix_ref):
    m = x_ref.shape[0] * x_ref.shape[1]
    return _rms(x_ref[...].reshape(m, x_ref.shape[-1]), gmix_ref[...]).astype(BF16)


def _hgrn_project_cols(xn, win_ref, lb_row, act, c0):
    vdim = HGRN_HEADS * HGRN_HEAD_DIM
    blk = _dot(xn, win_ref[:, c0:c0 + PROJ_COLS])
    sect, s0 = divmod(c0, vdim)
    if sect == 0:
        act[:, s0:s0 + PROJ_COLS] = blk * jax.nn.sigmoid(blk)
    elif sect == 1:
        lb = lb_row[:, s0:s0 + PROJ_COLS]
        fgate = lb + (1.0 - lb) * jax.nn.sigmoid(blk)
        act[:, vdim + s0:vdim + s0 + PROJ_COLS] = fgate
        act[:, 2 * vdim + s0:2 * vdim + s0 + PROJ_COLS] = jnp.log2(fgate)
    elif sect == 2:
        act[:, 3 * vdim + s0:3 * vdim + s0 + PROJ_COLS] = blk
    else:
        act[:, 4 * vdim + s0:4 * vdim + s0 + PROJ_COLS] = jax.nn.sigmoid(blk)


def _hgrn_kernel(x_ref, xnext_ref, s0_ref, gmix_ref, win_ref, lbp_ref, hn_ref, wout_ref, y_ref, sout_ref,
                 act_scr, yg_scr, st_scr=None, *, bb, tb, chunk, layer_slot):
    t = pl.program_id(1)
    step = pl.program_id(0) * pl.num_programs(1) + t
    slot = lax.rem(step, 2)
    d = x_ref.shape[-1]
    m = bb * tb
    dk = HGRN_HEAD_DIM
    single_time_block = st_scr is None

    if not single_time_block:
        @pl.when(t == 0)
        def _():
            st_scr[...] = s0_ref[...]

    lbp = lbp_ref[...]
    e = jnp.exp(lbp - jnp.max(lbp, axis=0, keepdims=True))
    sm = e / jnp.sum(e, axis=0, keepdims=True)
    lb_row = sm[0:1]
    for s in range(1, layer_slot + 1):
        lb_row = lb_row + sm[s:s + 1]

    proj_cols = list(range(0, 4 * HGRN_HEADS * dk, PROJ_COLS))

    @pl.when(step == 0)
    def _():
        xn0 = _hgrn_normed(x_ref, gmix_ref)
        for c0 in proj_cols:
            _hgrn_project_cols(xn0, win_ref, lb_row, act_scr.at[0], c0)

    def run(act, act_next):
        xn_next = _hgrn_normed(xnext_ref, gmix_ref)
        hn_row = hn_ref[...]
        orow = lax.broadcasted_iota(jnp.int32, (chunk + BF16_ROWS, dk), 0)
        ones_rows_bf = jnp.where((orow >= chunk) & (orow < chunk + 3), 1.0, 0.0).astype(BF16)
        chunks_per_b = tb // chunk
        n_chunks = m // chunk
        for ci in range(n_chunks):
            for c0 in proj_cols[ci * len(proj_cols) // n_chunks:(ci + 1) * len(proj_cols) // n_chunks]:
                _hgrn_project_cols(xn_next, win_ref, lb_row, act_next, c0)
            bi = ci // chunks_per_b
            if single_time_block:
                state_refs = [(s0_ref.at[bi, h], sout_ref.at[bi, h]) for h in range(HGRN_HEADS)]
            else:
                state_refs = [(st_scr.at[bi, h], st_scr.at[bi, h]) for h in range(HGRN_HEADS)]
            _hgrn_chunk(act, yg_scr, state_refs, slice(ci * chunk, (ci + 1) * chunk), hn_row, ones_rows_bf, chunk)
        y = _dot(yg_scr[...].astype(BF16), wout_ref[...])
        y_ref[...] = (x_ref[...].reshape(m, d) + y).reshape(bb, tb, d)

    for parity in (0, 1):
        @pl.when(slot == parity)
        def _():
            run(act_scr.at[parity], act_scr.at[1 - parity])

    if not single_time_block:
        @pl.when(t == pl.num_programs(1) - 1)
        def _():
            sout_ref[...] = st_scr[...]


def _hgrn(x, s0, gmix, win, lb_params, hnorm, wout, layer, *, layer_slot, bb, tb):
    b, t, d = x.shape
    chunk = min(HGRN_MAX_CHUNK, t)
    assert t % tb == 0 and tb % chunk == 0 and b % bb == 0 and chunk % SUBLANES == 0
    heads, dk = HGRN_HEADS, HGRN_HEAD_DIM
    vdim = heads * dk
    assert win.shape[2] == 4 * vdim and vdim % PROJ_COLS == 0
    m = bb * tb
    nj = t // tb
    n_steps = (b // bb) * nj

    def next_block(i, j):
        s = jnp.minimum(i * nj + j + 1, n_steps - 1)
        return (s // nj, s % nj, 0)

    scratch = [pltpu.VMEM((2, m, 5 * vdim), F32), pltpu.VMEM((m, vdim), F32)]
    if nj > 1:
        scratch.append(pltpu.VMEM((bb, heads, dk, dk), F32))
    return pl.pallas_call(
        functools.partial(_hgrn_kernel, bb=bb, tb=tb, chunk=chunk, layer_slot=layer_slot),
        grid=(b // bb, nj),
        in_specs=[
            pl.BlockSpec((bb, tb, d), lambda i, j: (i, j, 0)),
            pl.BlockSpec((bb, tb, d), next_block),
            pl.BlockSpec((None, bb, heads, dk, dk), lambda i, j: (layer_slot, i, 0, 0, 0)),
            _layer_spec(gmix, layer, 2),
            _layer_spec(win, layer_slot, 2),
            pl.BlockSpec(lb_params.shape, lambda i, j: (0, 0)),
            _layer_spec(hnorm, layer_slot, 2),
            _layer_spec(wout, layer_slot, 2),
        ],
        out_specs=[
            pl.BlockSpec((bb, tb, d), lambda i, j: (i, j, 0)),
            pl.BlockSpec((bb, heads, dk, dk), lambda i, j: (i, 0, 0, 0)),
        ],
        out_shape=[
            jax.ShapeDtypeStruct((b, t, d), F32),
            jax.ShapeDtypeStruct((b, heads, dk, dk), F32),
        ],
        scratch_shapes=scratch,
        compiler_params=pltpu.CompilerParams(
            dimension_semantics=("arbitrary", "arbitrary"), vmem_limit_bytes=VMEM_LIMIT_BYTES),
        name="hgrn_mixer",
    )(x, x, s0, gmix, win, lb_params, hnorm, wout)


def _conv_kernel(x_ref, buf_ref, gmix_ref, win_ref, cw_ref, wout_ref, y_ref, bufo_ref, ext_scr, *, bb, tb):
    t = pl.program_id(1)
    d = x_ref.shape[-1]
    m = bb * tb
    halo = SUBLANES
    tail = CONV_W - 1

    @pl.when(t == 0)
    def _():
        ext_scr[:, halo - tail:halo, :] = buf_ref[...]

    x = x_ref[...].reshape(m, d)
    xn = _rms(x, gmix_ref[...]).astype(BF16)
    b_gate = _dot(xn, win_ref[:, 0:d])
    c_gate = _dot(xn, win_ref[:, d:2 * d])
    v = _dot(xn, win_ref[:, 2 * d:3 * d])
    ext_scr[:, halo:halo + tb, :] = (c_gate * v).reshape(bb, tb, d)

    cw = cw_ref[...]
    conv = None
    for tap in range(CONV_W):
        start = halo - tail + tap
        term = cw[tap:tap + 1] * ext_scr[:, start:start + tb, :]
        conv = term if conv is None else conv + term
    y = _dot((b_gate * conv.reshape(m, d)).astype(BF16), wout_ref[...])
    y_ref[...] = (x + y).reshape(bb, tb, d)

    new_tail = ext_scr[:, halo + tb - tail:halo + tb, :]
    ext_scr[:, halo - tail:halo, :] = new_tail

    @pl.when(t == pl.num_programs(1) - 1)
    def _():
        bufo_ref[...] = new_tail


def _conv(x, buf, gmix, win, cw, wout, layer, *, layer_slot, bb, tb):
    b, t, d = x.shape
    tail = CONV_W - 1
    assert t % tb == 0 and b % bb == 0 and tb >= tail
    return pl.pallas_call(
        functools.partial(_conv_kernel, bb=bb, tb=tb),
        grid=(b // bb, t // tb),
        in_specs=[
            pl.BlockSpec((bb, tb, d), lambda i, j: (i, j, 0)),
            pl.BlockSpec((None, bb, tail, d), lambda i, j: (layer_slot, i, 0, 0)),
            _layer_spec(gmix, layer, 2),
            _layer_spec(win, layer_slot, 2),
            _layer_spec(cw, layer_slot, 2),
            _layer_spec(wout, layer_slot, 2),
        ],
        out_specs=[
            pl.BlockSpec((bb, tb, d), lambda i, j: (i, j, 0)),
            pl.BlockSpec((bb, tail, d), lambda i, j: (i, 0, 0)),
        ],
        out_shape=[
            jax.ShapeDtypeStruct((b, t, d), F32),
            jax.ShapeDtypeStruct((b, tail, d), F32),
        ],
        scratch_shapes=[pltpu.VMEM((bb, tb + SUBLANES, d), F32)],
        compiler_params=pltpu.CompilerParams(
            dimension_semantics=("arbitrary", "arbitrary"), vmem_limit_bytes=VMEM_LIMIT_BYTES),
        name="conv_mixer",
    )(x, buf, gmix, win, cw, wout)


def _trunk(x, state_hgrn, state_conv, p, *, ffn_tm, hgrn_tiles, conv_tiles):
    b, t, d = x.shape
    depth = p["norm_ffn1"].shape[0]
    new_hgrn, new_conv = [], []
    xf = x.reshape(b * t, d)
    for layer in range(depth):
        xf = _ffn(xf, p["norm_ffn1"], p["w_ffn1_up"], p["w_ffn1_down"], p["norm_final"], layer,
                  final_norm=False, tm=ffn_tm)
        j = layer // 2
        if layer % 2 == 0:
            x3, s = _hgrn(xf.reshape(b, t, d), state_hgrn, p["norm_mix"], p["w_hgrn_in"],
                          p["hgrn_lower_bounds"], p["hgrn_norm"], p["w_hgrn_out"], layer,
                          layer_slot=j, bb=hgrn_tiles[0], tb=hgrn_tiles[1])
            new_hgrn.append(s)
        else:
            x3, s = _conv(xf.reshape(b, t, d), state_conv, p["norm_mix"], p["w_conv_in"],
                          p["conv_w"], p["w_conv_out"], layer, layer_slot=j, bb=conv_tiles[0], tb=conv_tiles[1])
            new_conv.append(s)
        xf = _ffn(x3.reshape(b * t, d), p["norm_ffn2"], p["w_ffn2_up"], p["w_ffn2_down"], p["norm_final"], layer,
                  final_norm=(layer == depth - 1), tm=ffn_tm)
    return xf.reshape(b, t, d), jnp.stack(new_hgrn), jnp.stack(new_conv)


def kernel(x_prompt, x_sample, state_hgrn, state_conv, norm_ffn1, w_ffn1_up, w_ffn1_down, norm_mix, norm_ffn2, w_ffn2_up, w_ffn2_down, w_hgrn_in, hgrn_lower_bounds, hgrn_norm, w_hgrn_out, w_conv_in, conv_w, w_conv_out, norm_final):
    p = {
        "norm_ffn1": _rows(norm_ffn1), "norm_mix": _rows(norm_mix), "norm_ffn2": _rows(norm_ffn2),
        "norm_final": norm_final,
        "w_ffn1_up": w_ffn1_up.astype(BF16), "w_ffn1_down": w_ffn1_down.astype(BF16),
        "w_ffn2_up": w_ffn2_up.astype(BF16), "w_ffn2_down": w_ffn2_down.astype(BF16),
        "w_hgrn_in": w_hgrn_in.astype(BF16), "w_hgrn_out": w_hgrn_out.astype(BF16),
        "hgrn_lower_bounds": hgrn_lower_bounds.astype(F32), "hgrn_norm": _rows(hgrn_norm),
        "w_conv_in": w_conv_in.astype(BF16), "conv_w": conv_w, "w_conv_out": w_conv_out.astype(BF16),
    }
    bp = x_prompt.shape[0]
    n_hgrn, _, heads, dk, dv = state_hgrn.shape
    n_conv, _, tail, d = state_conv.shape
    zero_hgrn = jnp.zeros((n_hgrn, bp, heads, dk, dv), x_prompt.dtype)
    zero_conv = jnp.zeros((n_conv, bp, tail, d), x_prompt.dtype)

    y_p, sh_p, sc_p = _trunk(x_prompt, zero_hgrn, zero_conv, p,
                             ffn_tm=512, hgrn_tiles=(1, 256), conv_tiles=(1, 512))
    y_s, sh_s, sc_s = _trunk(x_sample, state_hgrn, state_conv, p,
                             ffn_tm=512, hgrn_tiles=(8, x_sample.shape[1]), conv_tiles=(16, x_sample.shape[1]))
    return (y_p, y_s, sh_p, sh_s, sc_p, sc_s)
```

```python
import functools

import jax
import jax.numpy as jnp
from jax import lax
from jax.experimental import pallas as pl
from jax.experimental.pallas import tpu as pltpu

F32 = jnp.float32
BF16 = jnp.bfloat16

NORM_EPS = 1e-6
FFN_RES = 0.5
HGRN_HEADS = 8
HGRN_HEAD_DIM = 128
HGRN_MAX_CHUNK = 32
CONV_W = 3
SUBLANES = 8
BF16_ROWS = 16
LANES = 128
FFN_CHUNK = 256
PROJ_COLS = 512
VMEM_LIMIT_BYTES = 58 * 1024 * 1024


def _rms(x, gain):
    r = lax.rsqrt(jnp.mean(x * x, axis=-1, keepdims=True) + NORM_EPS)
    return (x * r) * gain


def _dot(a, b):
    return jnp.dot(a, b, preferred_element_type=F32)


def _dot_nt(a, b):
    return lax.dot_general(a, b, (((1,), (1,)), ((), ())), preferred_element_type=F32)


def _dot_tn(a, b):
    return lax.dot_general(a, b, (((0,), (0,)), ((), ())), preferred_element_type=F32)


def _layer_spec(w, layer, n_grid):
    zeros = (0,) * (w.ndim - 1)
    index_map = (lambda i: (layer,) + zeros) if n_grid == 1 else (lambda i, j: (layer,) + zeros)
    return pl.BlockSpec((None,) + w.shape[1:], index_map, pipeline_mode=pl.Buffered(1))


def _rows(p):
    return p.reshape(p.shape[0], 1, p.shape[1])


def _ffn_kernel(x_ref, g_ref, wup_ref, wdn_ref, gf_ref, o_ref, *, final_norm):
    x = x_ref[...]
    xn = _rms(x, g_ref[...]).astype(BF16)
    ff = wdn_ref.shape[0]
    acc = jnp.zeros_like(x)
    for c0 in range(0, ff, FFN_CHUNK):
        gate = _dot(xn, wup_ref[:, c0:c0 + FFN_CHUNK])
        up = _dot(xn, wup_ref[:, ff + c0:ff + c0 + FFN_CHUNK])
        act = (gate * jax.nn.sigmoid(gate) * up).astype(BF16)
        acc = acc + _dot(act, wdn_ref[c0:c0 + FFN_CHUNK, :])
    out = x + FFN_RES * acc
    if final_norm:
        out = _rms(out, gf_ref[...])
    o_ref[...] = out


def _ffn(x, gains, wup, wdn, gain_final, layer, *, final_norm, tm):
    n, d = x.shape
    ff = wdn.shape[1]
    assert n % tm == 0 and ff % FFN_CHUNK == 0 and wup.shape[2] == 2 * ff
    return pl.pallas_call(
        functools.partial(_ffn_kernel, final_norm=final_norm),
        grid=(n // tm,),
        in_specs=[
            pl.BlockSpec((tm, d), lambda i: (i, 0)),
            _layer_spec(gains, layer, 1),
            _layer_spec(wup, layer, 1),
            _layer_spec(wdn, layer, 1),
            pl.BlockSpec((1, d), lambda i: (0, 0)),
        ],
        out_specs=pl.BlockSpec((tm, d), lambda i: (i, 0)),
        out_shape=jax.ShapeDtypeStruct((n, d), F32),
        compiler_params=pltpu.CompilerParams(
            dimension_semantics=("arbitrary",), vmem_limit_bytes=VMEM_LIMIT_BYTES),
        name="ffn_final" if final_norm else "ffn",
    )(x, gains, wup, wdn, gain_final.reshape(1, d))


def _hgrn_near_pairs(act_scr, o_scr, rows, h):
    dk = HGRN_HEAD_DIM
    vdim = HGRN_HEADS * dk
    lo = h * dk
    q = act_scr[rows, lo:lo + dk]
    fg = act_scr[rows, vdim + lo:vdim + lo + dk]
    v = act_scr[rows, 3 * vdim + lo:3 * vdim + lo + dk]
    n_rows = q.shape[0]
    shape3 = (n_rows // SUBLANES, SUBLANES, dk)
    q3, fg3, v3 = q.reshape(shape3), fg.reshape(shape3), v.reshape(shape3)
    k3 = 1.0 - fg3
    ridx = lax.broadcasted_iota(jnp.int32, shape3, 1)
    fm = jnp.where(ridx == 0, 0.0, fg3)
    decay = fm
    o3 = jnp.sum(q3 * k3, axis=-1, keepdims=True) * v3
    for d in range(1, SUBLANES):
        if d > 1:
            decay = decay * pltpu.roll(fm, d - 1, 1)
        score = jnp.sum((q3 * decay) * pltpu.roll(k3, d, 1), axis=-1, keepdims=True)
        o3 = o3 + score * pltpu.roll(v3, d, 1)
    o_scr[rows, lo:lo + dk] = o3.reshape(n_rows, dk)


def _hgrn_operands(act_scr, rows, h, chunk):
    dk = HGRN_HEAD_DIM
    vdim = HGRN_HEADS * dk
    nb = chunk // SUBLANES
    lo = h * dk
    q = act_scr[rows, lo:lo + dk]
    fg = act_scr[rows, vdim + lo:vdim + lo + dk]
    lf = act_scr[rows, 2 * vdim + lo:2 * vdim + lo + dk]
    v = act_scr[rows, 3 * vdim + lo:3 * vdim + lo + dk]
    k = 1.0 - fg

    shape3 = (nb, SUBLANES, dk)
    q3, k3, v3 = q.reshape(shape3), k.reshape(shape3), v.reshape(shape3)
    ridx = lax.broadcasted_iota(jnp.int32, shape3, 1)

    w3 = lf.reshape(shape3)
    for s in (1, 2, 4):
        w3 = w3 + jnp.where(ridx >= s, pltpu.roll(w3, s, 1), 0.0)
    tot = w3[:, SUBLANES - 1:SUBLANES, :]
    offs = [jnp.zeros((1, 1, dk), F32)]
    for j in range(nb):
        offs.append(offs[-1] + tot[j:j + 1])

    q_blk = q3 * jnp.exp2(w3)
    k_blk = k3 * jnp.exp2(tot - w3)
    if nb > 1:
        q_in = q_blk * jnp.exp2(jnp.concatenate(offs[:nb], axis=0))
        k_end = k_blk * jnp.exp2(jnp.concatenate([offs[nb] - offs[j + 1] for j in range(nb)], axis=0))
    else:
        q_in, k_end = q_blk, k_blk

    k_st = v_st = far_mask = None
    if nb > 1:
        k_rows, v_rows, starts = [], [], []
        n_rows = 0
        for i in range(1, nb):
            starts.append(n_rows)
            for j in range(i):
                k_rows.append(k_blk[j] * jnp.exp2(offs[i][0] - offs[j + 1][0]))
                v_rows.append(v3[j])
                n_rows += SUBLANES
        pad = -n_rows % BF16_ROWS
        if pad:
            k_rows.append(jnp.zeros((pad, dk), F32))
            v_rows.append(jnp.zeros((pad, dk), F32))
        k_st = jnp.concatenate(k_rows, axis=0).astype(BF16)
        v_st = jnp.concatenate(v_rows, axis=0).astype(BF16)
        r_tot = n_rows + pad
        row_blk = lax.broadcasted_iota(jnp.int32, (chunk, r_tot), 0) // SUBLANES
        col = lax.broadcasted_iota(jnp.int32, (chunk, r_tot), 1)
        col_blk = jnp.where(col < n_rows, 1, 0)
        for s0 in starts[1:]:
            col_blk = col_blk + jnp.where(col >= s0, 1, 0)
        far_mask = row_blk == col_blk

    e_tot = jnp.exp2(offs[nb][0])
    e_hi = e_tot.astype(BF16).astype(F32)
    e_mid = (e_tot - e_hi).astype(BF16).astype(F32)
    e_lo = e_tot - e_hi - e_mid
    prow = lax.broadcasted_iota(jnp.int32, (BF16_ROWS, dk), 0)
    pieces = jnp.where(prow == 0, e_hi, jnp.where(prow == 1, e_mid, jnp.where(prow == 2, e_lo, 0.0)))
    lhs_state = jnp.concatenate([k_end.reshape(chunk, dk), pieces], axis=0).astype(BF16)
    v_pad = jnp.concatenate([v, jnp.zeros((BF16_ROWS, dk), F32)], axis=0).astype(BF16)
    return (q_in.reshape(chunk, dk).astype(BF16), q_blk.reshape(chunk, dk).astype(BF16),
            k_st, v_st, far_mask, lhs_state, v_pad)


def _hgrn_chunk(act_scr, yg_scr, state_refs, rows, hn_row, ones_rows_bf, chunk):
    all_heads = list(range(HGRN_HEADS))
    if chunk > SUBLANES:
        first = _hgrn_first_matmuls(act_scr, state_refs, rows, ones_rows_bf, chunk, all_heads)
        for h in all_heads:
            _hgrn_near_pairs(act_scr, yg_scr, rows, h)
    else:
        first = {}
        for h in all_heads:
            _hgrn_near_pairs(act_scr, yg_scr, rows, h)
            first.update(_hgrn_first_matmuls(act_scr, state_refs, rows, ones_rows_bf, chunk, [h]))
    dk = HGRN_HEAD_DIM
    vdim = HGRN_HEADS * dk
    for h in all_heads:
        o, p, v_st = first[h]
        o = o + yg_scr[rows, h * dk:(h + 1) * dk]
        if p is not None:
            o = o + _dot(p, v_st)
        ms = jnp.mean(o * o, axis=-1, keepdims=True)
        gs = act_scr[rows, 4 * vdim + h * dk:4 * vdim + (h + 1) * dk]
        yg_scr[rows, h * dk:(h + 1) * dk] = ((o * lax.rsqrt(ms + NORM_EPS)) * hn_row[:, h * dk:(h + 1) * dk]) * gs


def _hgrn_first_matmuls(act_scr, state_refs, rows, ones_rows_bf, chunk, heads):
    dk = HGRN_HEAD_DIM
    ops = {h: _hgrn_operands(act_scr, rows, h, chunk) for h in heads}
    st = {h: state_refs[h][0][...] for h in heads}
    o = {h: _dot(ops[h][0], st[h].astype(BF16)) for h in heads}
    p = {h: None for h in heads}
    if chunk > SUBLANES:
        p = {h: jnp.where(ops[h][4], _dot_nt(ops[h][1], ops[h][2]), 0.0).astype(BF16) for h in heads}
    for h in heads:
        res = _dot_tn(ops[h][5], jnp.concatenate([ops[h][6], ones_rows_bf], axis=1))
        state_refs[h][1][...] = res[:, dk:] * st[h] + res[:, :dk]
    return {h: (o[h], p[h], ops[h][3]) for h in heads}


def _hgrn_normed(x_ref, gmix_ref):
    return _rms(x_ref[...], gmix_ref[...]).astype(BF16)


def _hgrn_project_cols(xn, win_ref, lb_row, act, c0):
    vdim = HGRN_HEADS * HGRN_HEAD_DIM
    blk = _dot(xn, win_ref[:, c0:c0 + PROJ_COLS])
    sect, s0 = divmod(c0, vdim)
    if sect == 0:
        act[:, s0:s0 + PROJ_COLS] = blk * jax.nn.sigmoid(blk)
    elif sect == 1:
        lb = lb_row[:, s0:s0 + PROJ_COLS]
        fgate = lb + (1.0 - lb) * jax.nn.sigmoid(blk)
        act[:, vdim + s0:vdim + s0 + PROJ_COLS] = fgate
        act[:, 2 * vdim + s0:2 * vdim + s0 + PROJ_COLS] = jnp.log2(fgate)
    elif sect == 2:
        act[:, 3 * vdim + s0:3 * vdim + s0 + PROJ_COLS] = blk
    else:
        act[:, 4 * vdim + s0:4 * vdim + s0 + PROJ_COLS] = jax.nn.sigmoid(blk)


def _hgrn_kernel(x_ref, xnext_ref, s0_ref, gmix_ref, win_ref, lbp_ref, hn_ref, wout_ref, y_ref, sout_ref,
                 act_scr, yg_scr, st_scr=None, *, bb, tb, chunk, layer_slot):
    t = pl.program_id(1)
    step = pl.program_id(0) * pl.num_programs(1) + t
    slot = lax.rem(step, 2)
    d = x_ref.shape[-1]
    m = bb * tb
    dk = HGRN_HEAD_DIM
    single_time_block = st_scr is None

    if not single_time_block:
        @pl.when(t == 0)
        def _():
            st_scr[...] = s0_ref[...]

    lbp = lbp_ref[...]
    e = jnp.exp(lbp - jnp.max(lbp, axis=0, keepdims=True))
    sm = e / jnp.sum(e, axis=0, keepdims=True)
    lb_row = sm[0:1]
    for s in range(1, layer_slot + 1):
        lb_row = lb_row + sm[s:s + 1]

    proj_cols = list(range(0, 4 * HGRN_HEADS * dk, PROJ_COLS))

    @pl.when(step == 0)
    def _():
        xn0 = _hgrn_normed(x_ref, gmix_ref)
        for c0 in proj_cols:
            _hgrn_project_cols(xn0, win_ref, lb_row, act_scr.at[0], c0)

    def run(act, act_next):
        xn_next = _hgrn_normed(xnext_ref, gmix_ref)
        hn_row = hn_ref[...]
        orow = lax.broadcasted_iota(jnp.int32, (chunk + BF16_ROWS, dk), 0)
        ones_rows_bf = jnp.where((orow >= chunk) & (orow < chunk + 3), 1.0, 0.0).astype(BF16)
        chunks_per_b = tb // chunk
        n_chunks = m // chunk
        for ci in range(n_chunks):
            for c0 in proj_cols[ci * len(proj_cols) // n_chunks:(ci + 1) * len(proj_cols) // n_chunks]:
                _hgrn_project_cols(xn_next, win_ref, lb_row, act_next, c0)
            bi = ci // chunks_per_b
            if single_time_block:
                state_refs = [(s0_ref.at[bi, h], sout_ref.at[bi, h]) for h in range(HGRN_HEADS)]
            else:
                state_refs = [(st_scr.at[bi, h], st_scr.at[bi, h]) for h in range(HGRN_HEADS)]
            _hgrn_chunk(act, yg_scr, state_refs, slice(ci * chunk, (ci + 1) * chunk), hn_row, ones_rows_bf, chunk)
        y = _dot(yg_scr[...].astype(BF16), wout_ref[...])
        y_ref[...] = x_ref[...] + y

    for parity in (0, 1):
        @pl.when(slot == parity)
        def _():
            run(act_scr.at[parity], act_scr.at[1 - parity])

    if not single_time_block:
        @pl.when(t == pl.num_programs(1) - 1)
        def _():
            sout_ref[...] = st_scr[...]


def _hgrn(x, b, t, s0, gmix, win, lb_params, hnorm, wout, layer, *, layer_slot, bb, tb):
    d = x.shape[1]
    chunk = min(HGRN_MAX_CHUNK, t)
    assert t % tb == 0 and tb % chunk == 0 and b % bb == 0 and chunk % SUBLANES == 0
    heads, dk = HGRN_HEADS, HGRN_HEAD_DIM
    vdim = heads * dk
    assert win.shape[2] == 4 * vdim and vdim % PROJ_COLS == 0
    m = bb * tb
    nj = t // tb
    n_steps = (b // bb) * nj

    assert bb == 1 or tb == t
    this_block = lambda i, j: (i * nj + j, 0)
    next_block = lambda i, j: (jnp.minimum(i * nj + j + 1, n_steps - 1), 0)

    scratch = [pltpu.VMEM((2, m, 5 * vdim), F32), pltpu.VMEM((m, vdim), F32)]
    if nj > 1:
        scratch.append(pltpu.VMEM((bb, heads, dk, dk), F32))
    return pl.pallas_call(
        functools.partial(_hgrn_kernel, bb=bb, tb=tb, chunk=chunk, layer_slot=layer_slot),
        grid=(b // bb, nj),
        in_specs=[
            pl.BlockSpec((m, d), this_block),
            pl.BlockSpec((m, d), next_block),
            pl.BlockSpec((None, bb, heads, dk, dk), lambda i, j: (layer_slot, i, 0, 0, 0)),
            _layer_spec(gmix, layer, 2),
            _layer_spec(win, layer_slot, 2),
            pl.BlockSpec(lb_params.shape, lambda i, j: (0, 0)),
            _layer_spec(hnorm, layer_slot, 2),
            _layer_spec(wout, layer_slot, 2),
        ],
        out_specs=[
            pl.BlockSpec((m, d), this_block),
            pl.BlockSpec((bb, heads, dk, dk), lambda i, j: (i, 0, 0, 0)),
        ],
        out_shape=[
            jax.ShapeDtypeStruct((b * t, d), F32),
            jax.ShapeDtypeStruct((b, heads, dk, dk), F32),
        ],
        scratch_shapes=scratch,
        compiler_params=pltpu.CompilerParams(
            dimension_semantics=("arbitrary", "arbitrary"), vmem_limit_bytes=VMEM_LIMIT_BYTES),
        name="hgrn_mixer",
    )(x, x, s0, gmix, win, lb_params, hnorm, wout)


def _conv_kernel(x_ref, buf_ref, gmix_ref, win_ref, cw_ref, wout_ref, y_ref, bufo_ref, ext_scr, *, bb, tb):
    t = pl.program_id(1)
    d = x_ref.shape[-1]
    m = bb * tb
    halo = SUBLANES
    tail = CONV_W - 1

    @pl.when(t == 0)
    def _():
        ext_scr[:, halo - tail:halo, :] = buf_ref[...]

    x = x_ref[...].reshape(m, d)
    xn = _rms(x, gmix_ref[...]).astype(BF16)
    b_gate = _dot(xn, win_ref[:, 0:d])
    c_gate = _dot(xn, win_ref[:, d:2 * d])
    v = _dot(xn, win_ref[:, 2 * d:3 * d])
    ext_scr[:, halo:halo + tb, :] = (c_gate * v).reshape(bb, tb, d)

    cw = cw_ref[...]
    conv = None
    for tap in range(CONV_W):
        start = halo - tail + tap
        term = cw[tap:tap + 1] * ext_scr[:, start:start + tb, :]
        conv = term if conv is None else conv + term
    y = _dot((b_gate * conv.reshape(m, d)).astype(BF16), wout_ref[...])
    y_ref[...] = (x + y).reshape(bb, tb, d)

    new_tail = ext_scr[:, halo + tb - tail:halo + tb, :]
    ext_scr[:, halo - tail:halo, :] = new_tail

    @pl.when(t == pl.num_programs(1) - 1)
    def _():
        bufo_ref[...] = new_tail


def _conv(x, buf, gmix, win, cw, wout, layer, *, layer_slot, bb, tb):
    b, t, d = x.shape
    tail = CONV_W - 1
    assert t % tb == 0 and b % bb == 0 and tb >= tail
    return pl.pallas_call(
        functools.partial(_conv_kernel, bb=bb, tb=tb),
        grid=(b // bb, t // tb),
        in_specs=[
            pl.BlockSpec((bb, tb, d), lambda i, j: (i, j, 0)),
            pl.BlockSpec((None, bb, tail, d), lambda i, j: (layer_slot, i, 0, 0)),
            _layer_spec(gmix, layer, 2),
            _layer_spec(win, layer_slot, 2),
            _layer_spec(cw, layer_slot, 2),
            _layer_spec(wout, layer_slot, 2),
        ],
        out_specs=[
            pl.BlockSpec((bb, tb, d), lambda i, j: (i, j, 0)),
            pl.BlockSpec((bb, tail, d), lambda i, j: (i, 0, 0)),
        ],
        out_shape=[
            jax.ShapeDtypeStruct((b, t, d), F32),
            jax.ShapeDtypeStruct((b, tail, d), F32),
        ],
        scratch_shapes=[pltpu.VMEM((bb, tb + SUBLANES, d), F32)],
        compiler_params=pltpu.CompilerParams(
            dimension_semantics=("arbitrary", "arbitrary"), vmem_limit_bytes=VMEM_LIMIT_BYTES),
        name="conv_mixer",
    )(x, buf, gmix, win, cw, wout)


def _trunk(x, state_hgrn, state_conv, p, *, ffn_tm, hgrn_tiles, conv_tiles):
    b, t, d = x.shape
    depth = p["norm_ffn1"].shape[0]
    new_hgrn, new_conv = [], []
    xf = x.reshape(b * t, d)
    for layer in range(depth):
        xf = _ffn(xf, p["norm_ffn1"], p["w_ffn1_up"], p["w_ffn1_down"], p["norm_final"], layer,
                  final_norm=False, tm=ffn_tm)
        j = layer // 2
        if layer % 2 == 0:
            x3, s = _hgrn(xf, b, t, state_hgrn, p["norm_mix"], p["w_hgrn_in"],
                          p["hgrn_lower_bounds"], p["hgrn_norm"], p["w_hgrn_out"], layer,
                          layer_slot=j, bb=hgrn_tiles[0], tb=hgrn_tiles[1])
            new_hgrn.append(s)
        else:
            x3, s = _conv(xf.reshape(b, t, d), state_conv, p["norm_mix"], p["w_conv_in"],
                          p["conv_w"], p["w_conv_out"], layer, layer_slot=j, bb=conv_tiles[0], tb=conv_tiles[1])
            new_conv.append(s)
        xf = _ffn(x3.reshape(b * t, d), p["norm_ffn2"], p["w_ffn2_up"], p["w_ffn2_down"], p["norm_final"], layer,
                  final_norm=(layer == depth - 1), tm=ffn_tm)
    return xf.reshape(b, t, d), jnp.stack(new_hgrn), jnp.stack(new_conv)


def kernel(x_prompt, x_sample, state_hgrn, state_conv, norm_ffn1, w_ffn1_up, w_ffn1_down, norm_mix, norm_ffn2, w_ffn2_up, w_ffn2_down, w_hgrn_in, hgrn_lower_bounds, hgrn_norm, w_hgrn_out, w_conv_in, conv_w, w_conv_out, norm_final):
    p = {
        "norm_ffn1": _rows(norm_ffn1), "norm_mix": _rows(norm_mix), "norm_ffn2": _rows(norm_ffn2),
        "norm_final": norm_final,
        "w_ffn1_up": w_ffn1_up.astype(BF16), "w_ffn1_down": w_ffn1_down.astype(BF16),
        "w_ffn2_up": w_ffn2_up.astype(BF16), "w_ffn2_down": w_ffn2_down.astype(BF16),
        "w_hgrn_in": w_hgrn_in.astype(BF16), "w_hgrn_out": w_hgrn_out.astype(BF16),
        "hgrn_lower_bounds": hgrn_lower_bounds.astype(F32), "hgrn_norm": _rows(hgrn_norm),
        "w_conv_in": w_conv_in.astype(BF16), "conv_w": conv_w, "w_conv_out": w_conv_out.astype(BF16),
    }
    bp = x_prompt.shape[0]
    n_hgrn, _, heads, dk, dv = state_hgrn.shape
    n_conv, _, tail, d = state_conv.shape
    zero_hgrn = jnp.zeros((n_hgrn, bp, heads, dk, dv), x_prompt.dtype)
    zero_conv = jnp.zeros((n_conv, bp, tail, d), x_prompt.dtype)

    y_p, sh_p, sc_p = _trunk(x_prompt, zero_hgrn, zero_conv, p,
                             ffn_tm=1024, hgrn_tiles=(1, 256), conv_tiles=(1, 512))
    y_s, sh_s, sc_s = _trunk(x_sample, state_hgrn, state_conv, p,
                             ffn_tm=512, hgrn_tiles=(8, x_sample.shape[1]), conv_tiles=(16, x_sample.shape[1]))
    return (y_p, y_s, sh_p, sh_s, sc_p, sc_s)
```

```python
import functools

import jax
import jax.numpy as jnp
from jax import lax
from jax.experimental import pallas as pl
from jax.experimental.pallas import tpu as pltpu

F32 = jnp.float32
BF16 = jnp.bfloat16

NORM_EPS = 1e-6
FFN_RES = 0.5
HGRN_HEADS = 8
HGRN_HEAD_DIM = 128
HGRN_MAX_CHUNK = 32
HGRN_HEAD_GROUP = 8
NEAR_FIRST_MATMUL_SHIFT = 8
CONV_W = 3
SUBLANES = 8
BF16_ROWS = 16
LANES = 128
FFN_CHUNK = 256
PROJ_COLS = 512
VMEM_LIMIT_BYTES = 58 * 1024 * 1024
FFN_ROWS = 512
HGRN_PROMPT_ROWS = 256
HGRN_SAMPLE_SEQS = 16
CONV_PROMPT_ROWS = 512
CONV_SAMPLE_SEQS = 16


def _rms(x, gain):
    r = lax.rsqrt(jnp.mean(x * x, axis=-1, keepdims=True) + NORM_EPS)
    return (x * r) * gain


def _dot(a, b):
    return jnp.dot(a, b, preferred_element_type=F32)


def _dot_nt(a, b):
    return lax.dot_general(a, b, (((1,), (1,)), ((), ())), preferred_element_type=F32)


def _dot_tn(a, b):
    return lax.dot_general(a, b, (((0,), (0,)), ((), ())), preferred_element_type=F32)


def _layer_spec(w, layer, n_grid):
    zeros = (0,) * (w.ndim - 1)
    index_map = (lambda i: (layer,) + zeros) if n_grid == 1 else (lambda i, j: (layer,) + zeros)
    return pl.BlockSpec((None,) + w.shape[1:], index_map, pipeline_mode=pl.Buffered(1))


def _rows(p):
    return p.reshape(p.shape[0], 1, p.shape[1])


def _ffn_kernel(xa_ref, xb_ref, g_ref, wup_ref, wdn_ref, gf_ref, oa_ref, ob_ref, *, n_a, final_norm):
    ff = wdn_ref.shape[0]

    def tile(x_ref, o_ref):
        x = x_ref[...]
        xn = _rms(x, g_ref[...]).astype(BF16)
        acc = jnp.zeros_like(x)
        for c0 in range(0, ff, FFN_CHUNK):
            gate = _dot(xn, wup_ref[:, c0:c0 + FFN_CHUNK])
            up = _dot(xn, wup_ref[:, ff + c0:ff + c0 + FFN_CHUNK])
            act = (gate * jax.nn.sigmoid(gate) * up).astype(BF16)
            acc = acc + _dot(act, wdn_ref[c0:c0 + FFN_CHUNK, :])
        out = x + FFN_RES * acc
        if final_norm:
            out = _rms(out, gf_ref[...])
        o_ref[...] = out

    on_a = pl.program_id(0) < n_a
    pl.when(on_a)(lambda: tile(xa_ref, oa_ref))
    pl.when(jnp.logical_not(on_a))(lambda: tile(xb_ref, ob_ref))


def _ffn(xa, xb, gains, wup, wdn, gain_final, layer, *, final_norm, tm):
    (na, d), nb = xa.shape, xb.shape[0]
    ff = wdn.shape[1]
    assert na % tm == 0 and nb % tm == 0 and ff % FFN_CHUNK == 0 and wup.shape[2] == 2 * ff
    n_a, n_b = na // tm, nb // tm
    a_block = lambda i: (jnp.minimum(i, n_a - 1), 0)
    b_block = lambda i: (jnp.maximum(i - n_a, 0), 0)
    return pl.pallas_call(
        functools.partial(_ffn_kernel, n_a=n_a, final_norm=final_norm),
        grid=(n_a + n_b,),
        in_specs=[
            pl.BlockSpec((tm, d), a_block),
            pl.BlockSpec((tm, d), b_block),
            _layer_spec(gains, layer, 1),
            _layer_spec(wup, layer, 1),
            _layer_spec(wdn, layer, 1),
            pl.BlockSpec((1, d), lambda i: (0, 0)),
        ],
        out_specs=[pl.BlockSpec((tm, d), a_block), pl.BlockSpec((tm, d), b_block)],
        out_shape=[jax.ShapeDtypeStruct((na, d), F32), jax.ShapeDtypeStruct((nb, d), F32)],
        compiler_params=pltpu.CompilerParams(
            dimension_semantics=("arbitrary",), vmem_limit_bytes=VMEM_LIMIT_BYTES),
        name="ffn_final" if final_norm else "ffn",
    )(xa, xb, gains, wup, wdn, gain_final.reshape(1, d))


def _hgrn_near_pairs(act_scr, o_scr, rows, h, ones_bf):
    dk = HGRN_HEAD_DIM
    vdim = HGRN_HEADS * dk
    lo = h * dk
    q = act_scr[rows, lo:lo + dk]
    fg = act_scr[rows, vdim + lo:vdim + lo + dk]
    v = act_scr[rows, 3 * vdim + lo:3 * vdim + lo + dk]
    n_rows = q.shape[0]
    shape3 = (n_rows // SUBLANES, SUBLANES, dk)
    q3, fg3, v3 = q.reshape(shape3), fg.reshape(shape3), v.reshape(shape3)
    k3 = 1.0 - fg3
    ridx = lax.broadcasted_iota(jnp.int32, shape3, 1)
    fm = jnp.where(ridx == 0, 0.0, fg3)
    decay = fm
    o3 = jnp.sum(q3 * k3, axis=-1, keepdims=True) * v3
    matmul_terms = []
    for d in range(1, SUBLANES):
        if d > 1:
            decay = decay * pltpu.roll(fm, d - 1, 1)
        term = (q3 * decay) * pltpu.roll(k3, d, 1)
        if d < NEAR_FIRST_MATMUL_SHIFT:
            o3 = o3 + jnp.sum(term, axis=-1, keepdims=True) * pltpu.roll(v3, d, 1)
        else:
            matmul_terms.append(term)
    if matmul_terms:
        stacked = jnp.concatenate(matmul_terms, axis=0).reshape(len(matmul_terms) * n_rows, dk)
        sums = _dot(stacked.astype(BF16), ones_bf).reshape((len(matmul_terms),) + shape3)
        for i in range(len(matmul_terms)):
            o3 = o3 + sums[i] * pltpu.roll(v3, NEAR_FIRST_MATMUL_SHIFT + i, 1)
    o_scr[rows, lo:lo + dk] = o3.reshape(n_rows, dk)


def _hgrn_operands(act_scr, rows, h, chunk):
    dk = HGRN_HEAD_DIM
    vdim = HGRN_HEADS * dk
    nb = chunk // SUBLANES
    lo = h * dk
    q = act_scr[rows, lo:lo + dk]
    fg = act_scr[rows, vdim + lo:vdim + lo + dk]
    lf = act_scr[rows, 2 * vdim + lo:2 * vdim + lo + dk]
    v = act_scr[rows, 3 * vdim + lo:3 * vdim + lo + dk]
    k = 1.0 - fg

    shape3 = (nb, SUBLANES, dk)
    q3, k3, v3 = q.reshape(shape3), k.reshape(shape3), v.reshape(shape3)
    ridx = lax.broadcasted_iota(jnp.int32, shape3, 1)

    w3 = lf.reshape(shape3)
    for s in (1, 2, 4):
        w3 = w3 + jnp.where(ridx >= s, pltpu.roll(w3, s, 1), 0.0)
    tot = w3[:, SUBLANES - 1:SUBLANES, :]
    offs = [jnp.zeros((1, 1, dk), F32)]
    for j in range(nb):
        offs.append(offs[-1] + tot[j:j + 1])

    q_blk = q3 * jnp.exp2(w3)
    k_blk = k3 * jnp.exp2(tot - w3)
    if nb > 1:
        q_in = q_blk * jnp.exp2(jnp.concatenate(offs[:nb], axis=0))
        k_end = k_blk * jnp.exp2(jnp.concatenate([offs[nb] - offs[j + 1] for j in range(nb)], axis=0))
    else:
        q_in, k_end = q_blk, k_blk

    k_st = v_st = far_mask = None
    if nb > 1:
        k_rows, v_rows, starts = [], [], []
        n_rows = 0
        for i in range(1, nb):
            starts.append(n_rows)
            for j in range(i):
                k_rows.append(k_blk[j] * jnp.exp2(offs[i][0] - offs[j + 1][0]))
                v_rows.append(v3[j])
                n_rows += SUBLANES
        pad = -n_rows % BF16_ROWS
        if pad:
            k_rows.append(jnp.zeros((pad, dk), F32))
            v_rows.append(jnp.zeros((pad, dk), F32))
        k_st = jnp.concatenate(k_rows, axis=0).astype(BF16)
        v_st = jnp.concatenate(v_rows, axis=0).astype(BF16)
        r_tot = n_rows + pad
        row_blk = lax.broadcasted_iota(jnp.int32, (chunk, r_tot), 0) // SUBLANES
        col = lax.broadcasted_iota(jnp.int32, (chunk, r_tot), 1)
        col_blk = jnp.where(col < n_rows, 1, 0)
        for s0 in starts[1:]:
            col_blk = col_blk + jnp.where(col >= s0, 1, 0)
        far_mask = row_blk == col_blk

    e_tot = jnp.exp2(offs[nb][0])
    e_hi = e_tot.astype(BF16).astype(F32)
    e_mid = (e_tot - e_hi).astype(BF16).astype(F32)
    e_lo = e_tot - e_hi - e_mid
    prow = lax.broadcasted_iota(jnp.int32, (BF16_ROWS, dk), 0)
    pieces = jnp.where(prow == 0, e_hi, jnp.where(prow == 1, e_mid, jnp.where(prow == 2, e_lo, 0.0)))
    lhs_state = jnp.concatenate([k_end.reshape(chunk, dk), pieces], axis=0).astype(BF16)
    v_pad = jnp.concatenate([v, jnp.zeros((BF16_ROWS, dk), F32)], axis=0).astype(BF16)
    return (q_in.reshape(chunk, dk).astype(BF16), q_blk.reshape(chunk, dk).astype(BF16),
            k_st, v_st, far_mask, lhs_state, v_pad)


def _hgrn_chunk(act_scr, yg_scr, state_refs, rows, hn_row, ones_bf, ones_rows_bf, chunk):
    dk = HGRN_HEAD_DIM
    vdim = HGRN_HEADS * dk

    def finish(h, o, p, v_st):
        o = o + yg_scr[rows, h * dk:(h + 1) * dk]
        if p is not None:
            o = o + _dot(p, v_st)
        ms = jnp.mean(o * o, axis=-1, keepdims=True)
        gs = act_scr[rows, 4 * vdim + h * dk:4 * vdim + (h + 1) * dk]
        yg_scr[rows, h * dk:(h + 1) * dk] = ((o * lax.rsqrt(ms + NORM_EPS)) * hn_row[:, h * dk:(h + 1) * dk]) * gs

    all_heads = list(range(HGRN_HEADS))
    if chunk > SUBLANES:
        for g0 in range(0, HGRN_HEADS, HGRN_HEAD_GROUP):
            group = all_heads[g0:g0 + HGRN_HEAD_GROUP]
            first = _hgrn_first_matmuls(act_scr, state_refs, rows, ones_rows_bf, chunk, group)
            for h in group:
                _hgrn_near_pairs(act_scr, yg_scr, rows, h, ones_bf)
            for h in group:
                finish(h, *first[h])
    else:
        first = {}
        for h in all_heads:
            _hgrn_near_pairs(act_scr, yg_scr, rows, h, ones_bf)
            first.update(_hgrn_first_matmuls(act_scr, state_refs, rows, ones_rows_bf, chunk, [h]))
        for h in all_heads:
            finish(h, *first[h])


def _hgrn_first_matmuls(act_scr, state_refs, rows, ones_rows_bf, chunk, heads):
    dk = HGRN_HEAD_DIM
    ops = {h: _hgrn_operands(act_scr, rows, h, chunk) for h in heads}
    st = {h: state_refs[h][0][...] for h in heads}
    o = {h: _dot(ops[h][0], st[h].astype(BF16)) for h in heads}
    p = {h: None for h in heads}
    if chunk > SUBLANES:
        p = {h: jnp.where(ops[h][4], _dot_nt(ops[h][1], ops[h][2]), 0.0).astype(BF16) for h in heads}
    for h in heads:
        res = _dot_tn(ops[h][5], jnp.concatenate([ops[h][6], ones_rows_bf], axis=1))
        state_refs[h][1][...] = res[:, dk:] * st[h] + res[:, :dk]
    return {h: (o[h], p[h], ops[h][3]) for h in heads}


def _hgrn_normed(x_ref, gmix_ref):
    return _rms(x_ref[...], gmix_ref[...]).astype(BF16)


def _hgrn_project_cols(xn, win_ref, lb_row, act, c0):
    vdim = HGRN_HEADS * HGRN_HEAD_DIM
    blk = _dot(xn, win_ref[:, c0:c0 + PROJ_COLS])
    sect, s0 = divmod(c0, vdim)
    if sect == 0:
        act[:, s0:s0 + PROJ_COLS] = blk * jax.nn.sigmoid(blk)
    elif sect == 1:
        lb = lb_row[:, s0:s0 + PROJ_COLS]
        fgate = lb + (1.0 - lb) * jax.nn.sigmoid(blk)
        act[:, vdim + s0:vdim + s0 + PROJ_COLS] = fgate
        act[:, 2 * vdim + s0:2 * vdim + s0 + PROJ_COLS] = jnp.log2(fgate)
    elif sect == 2:
        act[:, 3 * vdim + s0:3 * vdim + s0 + PROJ_COLS] = blk
    else:
        act[:, 4 * vdim + s0:4 * vdim + s0 + PROJ_COLS] = jax.nn.sigmoid(blk)


def _hgrn_kernel(x_ref, xnext_ref, s0_ref, gmix_ref, win_ref, lbp_ref, hn_ref, wout_ref, y_ref, sout_ref,
                 act_scr, yg_scr, st_scr=None, *, bb, tb, chunk, layer_slot):
    t = pl.program_id(1)
    step = pl.program_id(0) * pl.num_programs(1) + t
    slot = lax.rem(step, 2)
    d = x_ref.shape[-1]
    m = bb * tb
    dk = HGRN_HEAD_DIM
    single_time_block = st_scr is None

    if not single_time_block:
        @pl.when(t == 0)
        def _():
            st_scr[...] = s0_ref[...]

    lbp = lbp_ref[...]
    e = jnp.exp(lbp - jnp.max(lbp, axis=0, keepdims=True))
    sm = e / jnp.sum(e, axis=0, keepdims=True)
    lb_row = sm[0:1]
    for s in range(1, layer_slot + 1):
        lb_row = lb_row + sm[s:s + 1]

    proj_cols = list(range(0, 4 * HGRN_HEADS * dk, PROJ_COLS))

    @pl.when(step == 0)
    def _():
        xn0 = _hgrn_normed(x_ref, gmix_ref)
        for c0 in proj_cols:
            _hgrn_project_cols(xn0, win_ref, lb_row, act_scr.at[0], c0)

    def run(act, act_next):
        xn_next = _hgrn_normed(xnext_ref, gmix_ref)
        hn_row = hn_ref[...]
        orow = lax.broadcasted_iota(jnp.int32, (chunk + BF16_ROWS, dk), 0)
        ones_rows_bf = jnp.where((orow >= chunk) & (orow < chunk + 3), 1.0, 0.0).astype(BF16)
        chunks_per_b = tb // chunk
        n_chunks = m // chunk
        for ci in range(n_chunks):
            for c0 in proj_cols[ci * len(proj_cols) // n_chunks:(ci + 1) * len(proj_cols) // n_chunks]:
                _hgrn_project_cols(xn_next, win_ref, lb_row, act_next, c0)
            bi = ci // chunks_per_b
            if single_time_block:
                state_refs = [(s0_ref.at[bi, h], sout_ref.at[bi, h]) for h in range(HGRN_HEADS)]
            else:
                state_refs = [(st_scr.at[bi, h], st_scr.at[bi, h]) for h in range(HGRN_HEADS)]
            _hgrn_chunk(act, yg_scr, state_refs, slice(ci * chunk, (ci + 1) * chunk), hn_row,
                        jnp.ones((dk, dk), BF16), ones_rows_bf, chunk)
        y = _dot(yg_scr[...].astype(BF16), wout_ref[...])
        y_ref[...] = x_ref[...] + y

    for parity in (0, 1):
        @pl.when(slot == parity)
        def _():
            run(act_scr.at[parity], act_scr.at[1 - parity])

    if not single_time_block:
        @pl.when(t == pl.num_programs(1) - 1)
        def _():
            sout_ref[...] = st_scr[...]


def _hgrn(x, b, t, s0, gmix, win, lb_params, hnorm, wout, layer, *, layer_slot, bb, tb):
    d = x.shape[1]
    chunk = min(HGRN_MAX_CHUNK, t)
    assert t % tb == 0 and tb % chunk == 0 and b % bb == 0 and chunk % SUBLANES == 0
    heads, dk = HGRN_HEADS, HGRN_HEAD_DIM
    vdim = heads * dk
    assert win.shape[2] == 4 * vdim and vdim % PROJ_COLS == 0
    m = bb * tb
    nj = t // tb
    n_steps = (b // bb) * nj

    assert bb == 1 or tb == t
    this_block = lambda i, j: (i * nj + j, 0)
    next_block = lambda i, j: (jnp.minimum(i * nj + j + 1, n_steps - 1), 0)

    scratch = [pltpu.VMEM((2, m, 5 * vdim), F32), pltpu.VMEM((m, vdim), F32)]
    if nj > 1:
        scratch.append(pltpu.VMEM((bb, heads, dk, dk), F32))
    return pl.pallas_call(
        functools.partial(_hgrn_kernel, bb=bb, tb=tb, chunk=chunk, layer_slot=layer_slot),
        grid=(b // bb, nj),
        in_specs=[
            pl.BlockSpec((m, d), this_block),
            pl.BlockSpec((m, d), next_block),
            pl.BlockSpec((None, bb, heads, dk, dk), lambda i, j: (layer_slot, i, 0, 0, 0)),
            _layer_spec(gmix, layer, 2),
            _layer_spec(win, layer_slot, 2),
            pl.BlockSpec(lb_params.shape, lambda i, j: (0, 0)),
            _layer_spec(hnorm, layer_slot, 2),
            _layer_spec(wout, layer_slot, 2),
        ],
        out_specs=[
            pl.BlockSpec((m, d), this_block),
            pl.BlockSpec((bb, heads, dk, dk), lambda i, j: (i, 0, 0, 0)),
        ],
        out_shape=[
            jax.ShapeDtypeStruct((b * t, d), F32),
            jax.ShapeDtypeStruct((b, heads, dk, dk), F32),
        ],
        scratch_shapes=scratch,
        compiler_params=pltpu.CompilerParams(
            dimension_semantics=("arbitrary", "arbitrary"), vmem_limit_bytes=VMEM_LIMIT_BYTES),
        name="hgrn_mixer",
    )(x, x, s0, gmix, win, lb_params, hnorm, wout)


def _conv_kernel(x_ref, buf_ref, gmix_ref, win_ref, cw_ref, wout_ref, y_ref, bufo_ref, ext_scr, *, bb, tb):
    t = pl.program_id(1)
    d = x_ref.shape[-1]
    m = bb * tb
    halo = SUBLANES
    tail = CONV_W - 1

    @pl.when(t == 0)
    def _():
        ext_scr[:, halo - tail:halo, :] = buf_ref[...]

    x = x_ref[...].reshape(m, d)
    xn = _rms(x, gmix_ref[...]).astype(BF16)
    b_gate = _dot(xn, win_ref[:, 0:d])
    c_gate = _dot(xn, win_ref[:, d:2 * d])
    v = _dot(xn, win_ref[:, 2 * d:3 * d])
    ext_scr[:, halo:halo + tb, :] = (c_gate * v).reshape(bb, tb, d)

    cw = cw_ref[...]
    conv = None
    for tap in range(CONV_W):
        start = halo - tail + tap
        term = cw[tap:tap + 1] * ext_scr[:, start:start + tb, :]
        conv = term if conv is None else conv + term
    y = _dot((b_gate * conv.reshape(m, d)).astype(BF16), wout_ref[...])
    y_ref[...] = (x + y).reshape(bb, tb, d)

    new_tail = ext_scr[:, halo + tb - tail:halo + tb, :]
    ext_scr[:, halo - tail:halo, :] = new_tail

    @pl.when(t == pl.num_programs(1) - 1)
    def _():
        bufo_ref[...] = new_tail


def _conv(x, buf, gmix, win, cw, wout, layer, *, layer_slot, bb, tb):
    b, t, d = x.shape
    tail = CONV_W - 1
    assert t % tb == 0 and b % bb == 0 and tb >= tail
    return pl.pallas_call(
        functools.partial(_conv_kernel, bb=bb, tb=tb),
        grid=(b // bb, t // tb),
        in_specs=[
            pl.BlockSpec((bb, tb, d), lambda i, j: (i, j, 0)),
            pl.BlockSpec((None, bb, tail, d), lambda i, j: (layer_slot, i, 0, 0)),
            _layer_spec(gmix, layer, 2),
            _layer_spec(win, layer_slot, 2),
            _layer_spec(cw, layer_slot, 2),
            _layer_spec(wout, layer_slot, 2),
        ],
        out_specs=[
            pl.BlockSpec((bb, tb, d), lambda i, j: (i, j, 0)),
            pl.BlockSpec((bb, tail, d), lambda i, j: (i, 0, 0)),
        ],
        out_shape=[
            jax.ShapeDtypeStruct((b, t, d), F32),
            jax.ShapeDtypeStruct((b, tail, d), F32),
        ],
        scratch_shapes=[pltpu.VMEM((bb, tb + SUBLANES, d), F32)],
        compiler_params=pltpu.CompilerParams(
            dimension_semantics=("arbitrary", "arbitrary"), vmem_limit_bytes=VMEM_LIMIT_BYTES),
        name="conv_mixer",
    )(x, buf, gmix, win, cw, wout)


def kernel(x_prompt, x_sample, state_hgrn, state_conv, norm_ffn1, w_ffn1_up, w_ffn1_down, norm_mix, norm_ffn2, w_ffn2_up, w_ffn2_down, w_hgrn_in, hgrn_lower_bounds, hgrn_norm, w_hgrn_out, w_conv_in, conv_w, w_conv_out, norm_final):
    g_ffn1, g_mix, g_ffn2, g_hgrn = _rows(norm_ffn1), _rows(norm_mix), _rows(norm_ffn2), _rows(hgrn_norm)
    w1_up, w1_dn = w_ffn1_up.astype(BF16), w_ffn1_down.astype(BF16)
    w2_up, w2_dn = w_ffn2_up.astype(BF16), w_ffn2_down.astype(BF16)
    wh_in, wh_out = w_hgrn_in.astype(BF16), w_hgrn_out.astype(BF16)
    wc_in, wc_out = w_conv_in.astype(BF16), w_conv_out.astype(BF16)
    lb_params = hgrn_lower_bounds.astype(F32)

    (bp, tp, d), (bs, ts, _) = x_prompt.shape, x_sample.shape
    n_hgrn, _, heads, dk, dv = state_hgrn.shape
    n_conv, _, tail, _ = state_conv.shape
    depth = norm_ffn1.shape[0]
    zero_hgrn = jnp.zeros((n_hgrn, bp, heads, dk, dv), x_prompt.dtype)
    zero_conv = jnp.zeros((n_conv, bp, tail, d), x_prompt.dtype)

    xp, xs = x_prompt.reshape(bp * tp, d), x_sample.reshape(bs * ts, d)
    hgrn_p, hgrn_s, conv_p, conv_s = [], [], [], []
    for layer in range(depth):
        xp, xs = _ffn(xp, xs, g_ffn1, w1_up, w1_dn, norm_final, layer, final_norm=False, tm=FFN_ROWS)
        j = layer // 2
        if layer % 2 == 0:
            xp, s = _hgrn(xp, bp, tp, zero_hgrn, g_mix, wh_in, lb_params, g_hgrn, wh_out, layer,
                          layer_slot=j, bb=1, tb=HGRN_PROMPT_ROWS)
            hgrn_p.append(s)
            xs, s = _hgrn(xs, bs, ts, state_hgrn, g_mix, wh_in, lb_params, g_hgrn, wh_out, layer,
                          layer_slot=j, bb=HGRN_SAMPLE_SEQS, tb=ts)
            hgrn_s.append(s)
        else:
            x3, s = _conv(xp.reshape(bp, tp, d), zero_conv, g_mix, wc_in, conv_w, wc_out, layer,
                          layer_slot=j, bb=1, tb=CONV_PROMPT_ROWS)
            xp = x3.reshape(bp * tp, d)
            conv_p.append(s)
            x3, s = _conv(xs.reshape(bs, ts, d), state_conv, g_mix, wc_in, conv_w, wc_out, layer,
                          layer_slot=j, bb=CONV_SAMPLE_SEQS, tb=ts)
            xs = x3.reshape(bs * ts, d)
            conv_s.append(s)
        xp, xs = _ffn(xp, xs, g_ffn2, w2_up, w2_dn, norm_final, layer,
                      final_norm=(layer == depth - 1), tm=FFN_ROWS)
    return (xp.reshape(bp, tp, d), xs.reshape(bs, ts, d), jnp.stack(hgrn_p), jnp.stack(hgrn_s),
            jnp.stack(conv_p), jnp.stack(conv_s))
```

```python
import functools

import jax
import jax.numpy as jnp
from jax import lax
from jax.experimental import pallas as pl
from jax.experimental.pallas import tpu as pltpu

F32 = jnp.float32
BF16 = jnp.bfloat16

NORM_EPS = 1e-6
FFN_RES = 0.5
HGRN_HEADS = 8
HGRN_HEAD_DIM = 128
HGRN_MAX_CHUNK = 32
HGRN_HEAD_GROUP = 8
NEAR_FIRST_MATMUL_SHIFT = 8
CONV_W = 3
SUBLANES = 8
BF16_ROWS = 16
LANES = 128
FFN_CHUNK = 256
PROJ_COLS = 512
VMEM_LIMIT_BYTES = 58 * 1024 * 1024
FFN_ROWS = 512
HGRN_PROMPT_ROWS = 256
HGRN_SAMPLE_SEQS = 16
CONV_PROMPT_ROWS = 512
CONV_SAMPLE_SEQS = 16


def _rms(x, gain):
    r = lax.rsqrt(jnp.mean(x * x, axis=-1, keepdims=True) + NORM_EPS)
    return (x * r) * gain


def _dot(a, b):
    return jnp.dot(a, b, preferred_element_type=F32)


def _dot_nt(a, b):
    return lax.dot_general(a, b, (((1,), (1,)), ((), ())), preferred_element_type=F32)


def _dot_tn(a, b):
    return lax.dot_general(a, b, (((0,), (0,)), ((), ())), preferred_element_type=F32)


def _layer_spec(w, layer, n_grid):
    zeros = (0,) * (w.ndim - 1)
    index_map = (lambda i: (layer,) + zeros) if n_grid == 1 else (lambda i, j: (layer,) + zeros)
    return pl.BlockSpec((None,) + w.shape[1:], index_map, pipeline_mode=pl.Buffered(1))


def _rows(p):
    return p.reshape(p.shape[0], 1, p.shape[1])


def _ffn_kernel(xa_ref, xb_ref, g_ref, wup_ref, wdn_ref, gf_ref, oa_ref, ob_ref, *, n_a, final_norm):
    ff = wdn_ref.shape[0]

    def tile(x_ref, o_ref):
        x = x_ref[...]
        xn = _rms(x, g_ref[...]).astype(BF16)
        acc = jnp.zeros_like(x)
        for c0 in range(0, ff, FFN_CHUNK):
            gate = _dot(xn, wup_ref[:, c0:c0 + FFN_CHUNK].astype(BF16))
            up = _dot(xn, wup_ref[:, ff + c0:ff + c0 + FFN_CHUNK].astype(BF16))
            act = (gate * jax.nn.sigmoid(gate) * up).astype(BF16)
            acc = acc + _dot(act, wdn_ref[c0:c0 + FFN_CHUNK, :].astype(BF16))
        out = x + FFN_RES * acc
        if final_norm:
            out = _rms(out, gf_ref[...])
        o_ref[...] = out

    on_a = pl.program_id(0) < n_a
    pl.when(on_a)(lambda: tile(xa_ref, oa_ref))
    pl.when(jnp.logical_not(on_a))(lambda: tile(xb_ref, ob_ref))


def _ffn(xa, xb, gains, wup, wdn, gain_final, layer, *, final_norm, tm):
    (na, d), nb = xa.shape, xb.shape[0]
    ff = wdn.shape[1]
    assert na % tm == 0 and nb % tm == 0 and ff % FFN_CHUNK == 0 and wup.shape[2] == 2 * ff
    n_a, n_b = na // tm, nb // tm
    a_block = lambda i: (jnp.minimum(i, n_a - 1), 0)
    b_block = lambda i: (jnp.maximum(i - n_a, 0), 0)
    return pl.pallas_call(
        functools.partial(_ffn_kernel, n_a=n_a, final_norm=final_norm),
        grid=(n_a + n_b,),
        in_specs=[
            pl.BlockSpec((tm, d), a_block),
            pl.BlockSpec((tm, d), b_block),
            _layer_spec(gains, layer, 1),
            _layer_spec(wup, layer, 1),
            _layer_spec(wdn, layer, 1),
            pl.BlockSpec((1, d), lambda i: (0, 0)),
        ],
        out_specs=[pl.BlockSpec((tm, d), a_block), pl.BlockSpec((tm, d), b_block)],
        out_shape=[jax.ShapeDtypeStruct((na, d), F32), jax.ShapeDtypeStruct((nb, d), F32)],
        compiler_params=pltpu.CompilerParams(
            dimension_semantics=("arbitrary",), vmem_limit_bytes=VMEM_LIMIT_BYTES),
        name="ffn_final" if final_norm else "ffn",
    )(xa, xb, gains, wup, wdn, gain_final.reshape(1, d))


def _hgrn_near_pairs(act_scr, o_scr, rows, h, ones_bf):
    dk = HGRN_HEAD_DIM
    vdim = HGRN_HEADS * dk
    lo = h * dk
    q = act_scr[rows, lo:lo + dk]
    fg = act_scr[rows, vdim + lo:vdim + lo + dk]
    v = act_scr[rows, 3 * vdim + lo:3 * vdim + lo + dk]
    n_rows = q.shape[0]
    shape3 = (n_rows // SUBLANES, SUBLANES, dk)
    q3, fg3, v3 = q.reshape(shape3), fg.reshape(shape3), v.reshape(shape3)
    k3 = 1.0 - fg3
    ridx = lax.broadcasted_iota(jnp.int32, shape3, 1)
    fm = jnp.where(ridx == 0, 0.0, fg3)
    decay = fm
    o3 = jnp.sum(q3 * k3, axis=-1, keepdims=True) * v3
    matmul_terms = []
    for d in range(1, SUBLANES):
        if d > 1:
            decay = decay * pltpu.roll(fm, d - 1, 1)
        term = (q3 * decay) * pltpu.roll(k3, d, 1)
        if d < NEAR_FIRST_MATMUL_SHIFT:
            o3 = o3 + jnp.sum(term, axis=-1, keepdims=True) * pltpu.roll(v3, d, 1)
        else:
            matmul_terms.append(term)
    if matmul_terms:
        stacked = jnp.concatenate(matmul_terms, axis=0).reshape(len(matmul_terms) * n_rows, dk)
        sums = _dot(stacked.astype(BF16), ones_bf).reshape((len(matmul_terms),) + shape3)
        for i in range(len(matmul_terms)):
            o3 = o3 + sums[i] * pltpu.roll(v3, NEAR_FIRST_MATMUL_SHIFT + i, 1)
    o_scr[rows, lo:lo + dk] = o3.reshape(n_rows, dk)


def _hgrn_operands(act_scr, rows, h, chunk):
    dk = HGRN_HEAD_DIM
    vdim = HGRN_HEADS * dk
    nb = chunk // SUBLANES
    lo = h * dk
    q = act_scr[rows, lo:lo + dk]
    fg = act_scr[rows, vdim + lo:vdim + lo + dk]
    lf = act_scr[rows, 2 * vdim + lo:2 * vdim + lo + dk]
    v = act_scr[rows, 3 * vdim + lo:3 * vdim + lo + dk]
    k = 1.0 - fg

    shape3 = (nb, SUBLANES, dk)
    q3, k3, v3 = q.reshape(shape3), k.reshape(shape3), v.reshape(shape3)
    ridx = lax.broadcasted_iota(jnp.int32, shape3, 1)

    w3 = lf.reshape(shape3)
    for s in (1, 2, 4):
        w3 = w3 + jnp.where(ridx >= s, pltpu.roll(w3, s, 1), 0.0)
    tot = w3[:, SUBLANES - 1:SUBLANES, :]
    offs = [jnp.zeros((1, 1, dk), F32)]
    for j in range(nb):
        offs.append(offs[-1] + tot[j:j + 1])

    q_blk = q3 * jnp.exp2(w3)
    k_blk = k3 * jnp.exp2(tot - w3)
    if nb > 1:
        q_in = q_blk * jnp.exp2(jnp.concatenate(offs[:nb], axis=0))
        k_end = k_blk * jnp.exp2(jnp.concatenate([offs[nb] - offs[j + 1] for j in range(nb)], axis=0))
    else:
        q_in, k_end = q_blk, k_blk

    k_st = v_st = far_mask = None
    if nb > 1:
        k_rows, v_rows, starts = [], [], []
        n_rows = 0
        for i in range(1, nb):
            starts.append(n_rows)
            for j in range(i):
                k_rows.append(k_blk[j] * jnp.exp2(offs[i][0] - offs[j + 1][0]))
                v_rows.append(v3[j])
                n_rows += SUBLANES
        pad = -n_rows % BF16_ROWS
        if pad:
            k_rows.append(jnp.zeros((pad, dk), F32))
            v_rows.append(jnp.zeros((pad, dk), F32))
        k_st = jnp.concatenate(k_rows, axis=0).astype(BF16)
        v_st = jnp.concatenate(v_rows, axis=0).astype(BF16)
        r_tot = n_rows + pad
        row_blk = lax.broadcasted_iota(jnp.int32, (chunk, r_tot), 0) // SUBLANES
        col = lax.broadcasted_iota(jnp.int32, (chunk, r_tot), 1)
        col_blk = jnp.where(col < n_rows, 1, 0)
        for s0 in starts[1:]:
            col_blk = col_blk + jnp.where(col >= s0, 1, 0)
        far_mask = row_blk == col_blk

    e_tot = jnp.exp2(offs[nb][0])
    e_hi = e_tot.astype(BF16).astype(F32)
    e_mid = (e_tot - e_hi).astype(BF16).astype(F32)
    e_lo = e_tot - e_hi - e_mid
    prow = lax.broadcasted_iota(jnp.int32, (BF16_ROWS, dk), 0)
    pieces = jnp.where(prow == 0, e_hi, jnp.where(prow == 1, e_mid, jnp.where(prow == 2, e_lo, 0.0)))
    lhs_state = jnp.concatenate([k_end.reshape(chunk, dk), pieces], axis=0).astype(BF16)
    v_pad = jnp.concatenate([v, jnp.zeros((BF16_ROWS, dk), F32)], axis=0).astype(BF16)
    return (q_in.reshape(chunk, dk).astype(BF16), q_blk.reshape(chunk, dk).astype(BF16),
            k_st, v_st, far_mask, lhs_state, v_pad)


def _hgrn_chunk(act_scr, yg_scr, state_refs, rows, hn_row, ones_bf, ones_rows_bf, chunk):
    dk = HGRN_HEAD_DIM
    vdim = HGRN_HEADS * dk

    def finish(h, o, p, v_st):
        o = o + yg_scr[rows, h * dk:(h + 1) * dk]
        if p is not None:
            o = o + _dot(p, v_st)
        ms = jnp.mean(o * o, axis=-1, keepdims=True)
        gs = act_scr[rows, 4 * vdim + h * dk:4 * vdim + (h + 1) * dk]
        yg_scr[rows, h * dk:(h + 1) * dk] = ((o * lax.rsqrt(ms + NORM_EPS)) * hn_row[:, h * dk:(h + 1) * dk]) * gs

    all_heads = list(range(HGRN_HEADS))
    if chunk > SUBLANES:
        for g0 in range(0, HGRN_HEADS, HGRN_HEAD_GROUP):
            group = all_heads[g0:g0 + HGRN_HEAD_GROUP]
            first = _hgrn_first_matmuls(act_scr, state_refs, rows, ones_rows_bf, chunk, group)
            for h in group:
                _hgrn_near_pairs(act_scr, yg_scr, rows, h, ones_bf)
            for h in group:
                finish(h, *first[h])
    else:
        first = {}
        for h in all_heads:
            _hgrn_near_pairs(act_scr, yg_scr, rows, h, ones_bf)
            first.update(_hgrn_first_matmuls(act_scr, state_refs, rows, ones_rows_bf, chunk, [h]))
        for h in all_heads:
            finish(h, *first[h])


def _hgrn_first_matmuls(act_scr, state_refs, rows, ones_rows_bf, chunk, heads):
    dk = HGRN_HEAD_DIM
    ops = {h: _hgrn_operands(act_scr, rows, h, chunk) for h in heads}
    st = {h: state_refs[h][0][...] for h in heads}
    o = {h: _dot(ops[h][0], st[h].astype(BF16)) for h in heads}
    p = {h: None for h in heads}
    if chunk > SUBLANES:
        p = {h: jnp.where(ops[h][4], _dot_nt(ops[h][1], ops[h][2]), 0.0).astype(BF16) for h in heads}
    for h in heads:
        res = _dot_tn(ops[h][5], jnp.concatenate([ops[h][6], ones_rows_bf], axis=1))
        state_refs[h][1][...] = res[:, dk:] * st[h] + res[:, :dk]
    return {h: (o[h], p[h], ops[h][3]) for h in heads}


def _hgrn_normed(x_ref, gmix_ref):
    return _rms(x_ref[...], gmix_ref[...]).astype(BF16)


def _hgrn_project_cols(xn, win_ref, lb_row, act, c0):
    vdim = HGRN_HEADS * HGRN_HEAD_DIM
    blk = _dot(xn, win_ref[:, c0:c0 + PROJ_COLS])
    sect, s0 = divmod(c0, vdim)
    if sect == 0:
        act[:, s0:s0 + PROJ_COLS] = blk * jax.nn.sigmoid(blk)
    elif sect == 1:
        lb = lb_row[:, s0:s0 + PROJ_COLS]
        fgate = lb + (1.0 - lb) * jax.nn.sigmoid(blk)
        act[:, vdim + s0:vdim + s0 + PROJ_COLS] = fgate
        act[:, 2 * vdim + s0:2 * vdim + s0 + PROJ_COLS] = jnp.log2(fgate)
    elif sect == 2:
        act[:, 3 * vdim + s0:3 * vdim + s0 + PROJ_COLS] = blk
    else:
        act[:, 4 * vdim + s0:4 * vdim + s0 + PROJ_COLS] = jax.nn.sigmoid(blk)


def _hgrn_kernel(x_ref, xnext_ref, s0_ref, gmix_ref, win_ref, lbp_ref, hn_ref, wout_ref, y_ref, sout_ref,
                 act_scr, yg_scr, st_scr=None, *, bb, tb, chunk, layer_slot):
    t = pl.program_id(1)
    step = pl.program_id(0) * pl.num_programs(1) + t
    slot = lax.rem(step, 2)
    d = x_ref.shape[-1]
    m = bb * tb
    dk = HGRN_HEAD_DIM
    single_time_block = st_scr is None

    if not single_time_block:
        @pl.when(t == 0)
        def _():
            st_scr[...] = s0_ref[...]

    lbp = lbp_ref[...]
    e = jnp.exp(lbp - jnp.max(lbp, axis=0, keepdims=True))
    sm = e / jnp.sum(e, axis=0, keepdims=True)
    lb_row = sm[0:1]
    for s in range(1, layer_slot + 1):
        lb_row = lb_row + sm[s:s + 1]

    proj_cols = list(range(0, 4 * HGRN_HEADS * dk, PROJ_COLS))

    @pl.when(step == 0)
    def _():
        xn0 = _hgrn_normed(x_ref, gmix_ref)
        for c0 in proj_cols:
            _hgrn_project_cols(xn0, win_ref, lb_row, act_scr.at[0], c0)

    def run(act, act_next):
        xn_next = _hgrn_normed(xnext_ref, gmix_ref)
        hn_row = hn_ref[...]
        orow = lax.broadcasted_iota(jnp.int32, (chunk + BF16_ROWS, dk), 0)
        ones_rows_bf = jnp.where((orow >= chunk) & (orow < chunk + 3), 1.0, 0.0).astype(BF16)
        chunks_per_b = tb // chunk
        n_chunks = m // chunk
        for ci in range(n_chunks):
            for c0 in proj_cols[ci * len(proj_cols) // n_chunks:(ci + 1) * len(proj_cols) // n_chunks]:
                _hgrn_project_cols(xn_next, win_ref, lb_row, act_next, c0)
            bi = ci // chunks_per_b
            if single_time_block:
                state_refs = [(s0_ref.at[bi, h], sout_ref.at[bi, h]) for h in range(HGRN_HEADS)]
            else:
                state_refs = [(st_scr.at[bi, h], st_scr.at[bi, h]) for h in range(HGRN_HEADS)]
            _hgrn_chunk(act, yg_scr, state_refs, slice(ci * chunk, (ci + 1) * chunk), hn_row,
                        jnp.ones((dk, dk), BF16), ones_rows_bf, chunk)
        y = _dot(yg_scr[...].astype(BF16), wout_ref[...])
        y_ref[...] = x_ref[...] + y

    for parity in (0, 1):
        @pl.when(slot == parity)
        def _():
            run(act_scr.at[parity], act_scr.at[1 - parity])

    if not single_time_block:
        @pl.when(t == pl.num_programs(1) - 1)
        def _():
            sout_ref[...] = st_scr[...]


def _hgrn(x, b, t, s0, gmix, win, lb_params, hnorm, wout, layer, *, layer_slot, bb, tb):
    d = x.shape[1]
    chunk = min(HGRN_MAX_CHUNK, t)
    assert t % tb == 0 and tb % chunk == 0 and b % bb == 0 and chunk % SUBLANES == 0
    heads, dk = HGRN_HEADS, HGRN_HEAD_DIM
    vdim = heads * dk
    assert win.shape[2] == 4 * vdim and vdim % PROJ_COLS == 0
    m = bb * tb
    nj = t // tb
    n_steps = (b // bb) * nj

    assert bb == 1 or tb == t
    this_block = lambda i, j: (i * nj + j, 0)
    next_block = lambda i, j: (jnp.minimum(i * nj + j + 1, n_steps - 1), 0)

    scratch = [pltpu.VMEM((2, m, 5 * vdim), F32), pltpu.VMEM((m, vdim), F32)]
    if nj > 1:
        scratch.append(pltpu.VMEM((bb, heads, dk, dk), F32))
    return pl.pallas_call(
        functools.partial(_hgrn_kernel, bb=bb, tb=tb, chunk=chunk, layer_slot=layer_slot),
        grid=(b // bb, nj),
        in_specs=[
            pl.BlockSpec((m, d), this_block),
            pl.BlockSpec((m, d), next_block),
            pl.BlockSpec((None, bb, heads, dk, dk), lambda i, j: (layer_slot, i, 0, 0, 0)),
            _layer_spec(gmix, layer, 2),
            _layer_spec(win, layer_slot, 2),
            pl.BlockSpec(lb_params.shape, lambda i, j: (0, 0)),
            _layer_spec(hnorm, layer_slot, 2),
            _layer_spec(wout, layer_slot, 2),
        ],
        out_specs=[
            pl.BlockSpec((m, d), this_block),
            pl.BlockSpec((bb, heads, dk, dk), lambda i, j: (i, 0, 0, 0)),
        ],
        out_shape=[
            jax.ShapeDtypeStruct((b * t, d), F32),
            jax.ShapeDtypeStruct((b, heads, dk, dk), F32),
        ],
        scratch_shapes=scratch,
        compiler_params=pltpu.CompilerParams(
            dimension_semantics=("arbitrary", "arbitrary"), vmem_limit_bytes=VMEM_LIMIT_BYTES),
        name="hgrn_mixer",
    )(x, x, s0, gmix, win, lb_params, hnorm, wout)


def _conv_kernel(x_ref, buf_ref, gmix_ref, win_ref, cw_ref, wout_ref, y_ref, bufo_ref, ext_scr, *, bb, tb):
    t = pl.program_id(1)
    d = x_ref.shape[-1]
    m = bb * tb
    halo = SUBLANES
    tail = CONV_W - 1

    @pl.when(t == 0)
    def _():
        ext_scr[:, halo - tail:halo, :] = buf_ref[...]

    x = x_ref[...].reshape(m, d)
    xn = _rms(x, gmix_ref[...]).astype(BF16)
    b_gate = _dot(xn, win_ref[:, 0:d].astype(BF16))
    c_gate = _dot(xn, win_ref[:, d:2 * d].astype(BF16))
    v = _dot(xn, win_ref[:, 2 * d:3 * d].astype(BF16))
    ext_scr[:, halo:halo + tb, :] = (c_gate * v).reshape(bb, tb, d)

    cw = cw_ref[...]
    conv = None
    for tap in range(CONV_W):
        start = halo - tail + tap
        term = cw[tap:tap + 1] * ext_scr[:, start:start + tb, :]
        conv = term if conv is None else conv + term
    y = _dot((b_gate * conv.reshape(m, d)).astype(BF16), wout_ref[...].astype(BF16))
    y_ref[...] = (x + y).reshape(bb, tb, d)

    new_tail = ext_scr[:, halo + tb - tail:halo + tb, :]
    ext_scr[:, halo - tail:halo, :] = new_tail

    @pl.when(t == pl.num_programs(1) - 1)
    def _():
        bufo_ref[...] = new_tail


def _conv(x, buf, gmix, win, cw, wout, layer, *, layer_slot, bb, tb):
    b, t, d = x.shape
    tail = CONV_W - 1
    assert t % tb == 0 and b % bb == 0 and tb >= tail
    return pl.pallas_call(
        functools.partial(_conv_kernel, bb=bb, tb=tb),
        grid=(b // bb, t // tb),
        in_specs=[
            pl.BlockSpec((bb, tb, d), lambda i, j: (i, j, 0)),
            pl.BlockSpec((None, bb, tail, d), lambda i, j: (layer_slot, i, 0, 0)),
            _layer_spec(gmix, layer, 2),
            _layer_spec(win, layer_slot, 2),
            _layer_spec(cw, layer_slot, 2),
            _layer_spec(wout, layer_slot, 2),
        ],
        out_specs=[
            pl.BlockSpec((bb, tb, d), lambda i, j: (i, j, 0)),
            pl.BlockSpec((bb, tail, d), lambda i, j: (i, 0, 0)),
        ],
        out_shape=[
            jax.ShapeDtypeStruct((b, t, d), F32),
            jax.ShapeDtypeStruct((b, tail, d), F32),
        ],
        scratch_shapes=[pltpu.VMEM((bb, tb + SUBLANES, d), F32)],
        compiler_params=pltpu.CompilerParams(
            dimension_semantics=("arbitrary", "arbitrary"), vmem_limit_bytes=VMEM_LIMIT_BYTES),
        name="conv_mixer",
    )(x, buf, gmix, win, cw, wout)


def kernel(x_prompt, x_sample, state_hgrn, state_conv, norm_ffn1, w_ffn1_up, w_ffn1_down, norm_mix, norm_ffn2, w_ffn2_up, w_ffn2_down, w_hgrn_in, hgrn_lower_bounds, hgrn_norm, w_hgrn_out, w_conv_in, conv_w, w_conv_out, norm_final):
    g_ffn1, g_mix, g_ffn2, g_hgrn = _rows(norm_ffn1), _rows(norm_mix), _rows(norm_ffn2), _rows(hgrn_norm)
    w1_up, w1_dn, w2_up, w2_dn = w_ffn1_up, w_ffn1_down, w_ffn2_up, w_ffn2_down
    wc_in, wc_out = w_conv_in, w_conv_out
    wh_in, wh_out = w_hgrn_in.astype(BF16), w_hgrn_out.astype(BF16)
    lb_params = hgrn_lower_bounds.astype(F32)

    (bp, tp, d), (bs, ts, _) = x_prompt.shape, x_sample.shape
    n_hgrn, _, heads, dk, dv = state_hgrn.shape
    n_conv, _, tail, _ = state_conv.shape
    depth = norm_ffn1.shape[0]
    zero_hgrn = jnp.zeros((n_hgrn, bp, heads, dk, dv), x_prompt.dtype)
    zero_conv = jnp.zeros((n_conv, bp, tail, d), x_prompt.dtype)

    xp, xs = x_prompt.reshape(bp * tp, d), x_sample.reshape(bs * ts, d)
    hgrn_p, hgrn_s, conv_p, conv_s = [], [], [], []
    for layer in range(depth):
        xp, xs = _ffn(xp, xs, g_ffn1, w1_up, w1_dn, norm_final, layer, final_norm=False, tm=FFN_ROWS)
        j = layer // 2
        if layer % 2 == 0:
            xp, s = _hgrn(xp, bp, tp, zero_hgrn, g_mix, wh_in, lb_params, g_hgrn, wh_out, layer,
                          layer_slot=j, bb=1, tb=HGRN_PROMPT_ROWS)
            hgrn_p.append(s)
            xs, s = _hgrn(xs, bs, ts, state_hgrn, g_mix, wh_in, lb_params, g_hgrn, wh_out, layer,
                          layer_slot=j, bb=HGRN_SAMPLE_SEQS, tb=ts)
            hgrn_s.append(s)
        else:
            x3, s = _conv(xp.reshape(bp, tp, d), zero_conv, g_mix, wc_in, conv_w, wc_out, layer,
                          layer_slot=j, bb=1, tb=CONV_PROMPT_ROWS)
            xp = x3.reshape(bp * tp, d)
            conv_p.append(s)
            x3, s = _conv(xs.reshape(bs, ts, d), state_conv, g_mix, wc_in, conv_w, wc_out, layer,
                          layer_slot=j, bb=CONV_SAMPLE_SEQS, tb=ts)
            xs = x3.reshape(bs * ts, d)
            conv_s.append(s)
        xp, xs = _ffn(xp, xs, g_ffn2, w2_up, w2_dn, norm_final, layer,
                      final_norm=(layer == depth - 1), tm=FFN_ROWS)
    return (xp.reshape(bp, tp, d), xs.reshape(bs, ts, d), jnp.stack(hgrn_p), jnp.stack(hgrn_s),
            jnp.stack(conv_p), jnp.stack(conv_s))
```

```python
import functools

import jax
import jax.numpy as jnp
from jax import lax
from jax.experimental import pallas as pl
from jax.experimental.pallas import tpu as pltpu

F32 = jnp.float32
BF16 = jnp.bfloat16

NORM_EPS = 1e-6
FFN_RES = 0.5
HGRN_HEADS = 8
HGRN_HEAD_DIM = 128
HGRN_MAX_CHUNK = 32
HGRN_HEAD_GROUP = 8
NEAR_FIRST_MATMUL_SHIFT = 5
CONV_W = 3
SUBLANES = 8
BF16_ROWS = 16
LANES = 128
FFN_CHUNK = 256
PROJ_COLS = 512
VMEM_LIMIT_BYTES = 58 * 1024 * 1024
FFN_ROWS = 512
HGRN_PROMPT_ROWS = 256
HGRN_SAMPLE_SEQS = 16
CONV_PROMPT_ROWS = 512
CONV_SAMPLE_SEQS = 16


def _rms(x, gain):
    r = lax.rsqrt(jnp.mean(x * x, axis=-1, keepdims=True) + NORM_EPS)
    return (x * r) * gain


def _dot(a, b):
    return jnp.dot(a, b, preferred_element_type=F32)


def _dot_nt(a, b):
    return lax.dot_general(a, b, (((1,), (1,)), ((), ())), preferred_element_type=F32)


def _dot_tn(a, b):
    return lax.dot_general(a, b, (((0,), (0,)), ((), ())), preferred_element_type=F32)


def _layer_spec(w, layer, n_grid):
    zeros = (0,) * (w.ndim - 1)
    index_map = (lambda i: (layer,) + zeros) if n_grid == 1 else (lambda i, j: (layer,) + zeros)
    return pl.BlockSpec((None,) + w.shape[1:], index_map, pipeline_mode=pl.Buffered(1))


def _rows(p):
    return p.reshape(p.shape[0], 1, p.shape[1])


def _ffn_kernel(xa_ref, xb_ref, g_ref, wup_ref, wdn_ref, gf_ref, oa_ref, ob_ref, *, n_a, final_norm):
    ff = wdn_ref.shape[0]

    def tile(x_ref, o_ref):
        x = x_ref[...]
        xn = _rms(x, g_ref[...]).astype(BF16)
        acc = jnp.zeros_like(x)
        for c0 in range(0, ff, FFN_CHUNK):
            gate = _dot(xn, wup_ref[:, c0:c0 + FFN_CHUNK].astype(BF16))
            up = _dot(xn, wup_ref[:, ff + c0:ff + c0 + FFN_CHUNK].astype(BF16))
            act = (gate * jax.nn.sigmoid(gate) * up).astype(BF16)
            acc = acc + _dot(act, wdn_ref[c0:c0 + FFN_CHUNK, :].astype(BF16))
        out = x + FFN_RES * acc
        if final_norm:
            out = _rms(out, gf_ref[...])
        o_ref[...] = out

    on_a = pl.program_id(0) < n_a
    pl.when(on_a)(lambda: tile(xa_ref, oa_ref))
    pl.when(jnp.logical_not(on_a))(lambda: tile(xb_ref, ob_ref))


def _ffn(xa, xb, gains, wup, wdn, gain_final, layer, *, final_norm, tm):
    (na, d), nb = xa.shape, xb.shape[0]
    ff = wdn.shape[1]
    assert na % tm == 0 and nb % tm == 0 and ff % FFN_CHUNK == 0 and wup.shape[2] == 2 * ff
    n_a, n_b = na // tm, nb // tm
    a_block = lambda i: (jnp.minimum(i, n_a - 1), 0)
    b_block = lambda i: (jnp.maximum(i - n_a, 0), 0)
    return pl.pallas_call(
        functools.partial(_ffn_kernel, n_a=n_a, final_norm=final_norm),
        grid=(n_a + n_b,),
        in_specs=[
            pl.BlockSpec((tm, d), a_block),
            pl.BlockSpec((tm, d), b_block),
            _layer_spec(gains, layer, 1),
            _layer_spec(wup, layer, 1),
            _layer_spec(wdn, layer, 1),
            pl.BlockSpec((1, d), lambda i: (0, 0)),
        ],
        out_specs=[pl.BlockSpec((tm, d), a_block), pl.BlockSpec((tm, d), b_block)],
        out_shape=[jax.ShapeDtypeStruct((na, d), F32), jax.ShapeDtypeStruct((nb, d), F32)],
        compiler_params=pltpu.CompilerParams(
            dimension_semantics=("arbitrary",), vmem_limit_bytes=VMEM_LIMIT_BYTES),
        name="ffn_final" if final_norm else "ffn",
    )(xa, xb, gains, wup, wdn, gain_final.reshape(1, d))


def _hgrn_near_pairs(act_scr, o_scr, rows, h, ones_bf):
    dk = HGRN_HEAD_DIM
    vdim = HGRN_HEADS * dk
    lo = h * dk
    q = act_scr[rows, lo:lo + dk]
    fg = act_scr[rows, vdim + lo:vdim + lo + dk]
    v = act_scr[rows, 3 * vdim + lo:3 * vdim + lo + dk]
    n_rows = q.shape[0]
    shape3 = (n_rows // SUBLANES, SUBLANES, dk)
    q3, fg3, v3 = q.reshape(shape3), fg.reshape(shape3), v.reshape(shape3)
    k3 = 1.0 - fg3
    ridx = lax.broadcasted_iota(jnp.int32, shape3, 1)
    fm = jnp.where(ridx == 0, 0.0, fg3)
    decay = fm
    o3 = jnp.sum(q3 * k3, axis=-1, keepdims=True) * v3
    matmul_terms = []
    for d in range(1, SUBLANES):
        if d > 1:
            decay = decay * pltpu.roll(fm, d - 1, 1)
        term = (q3 * decay) * pltpu.roll(k3, d, 1)
        if d < NEAR_FIRST_MATMUL_SHIFT:
            o3 = o3 + jnp.sum(term, axis=-1, keepdims=True) * pltpu.roll(v3, d, 1)
        else:
            matmul_terms.append(term)
    if matmul_terms:
        stacked = jnp.concatenate(matmul_terms, axis=0).reshape(len(matmul_terms) * n_rows, dk)
        sums = _dot(stacked.astype(BF16), ones_bf).reshape((len(matmul_terms),) + shape3)
        for i in range(len(matmul_terms)):
            o3 = o3 + sums[i] * pltpu.roll(v3, NEAR_FIRST_MATMUL_SHIFT + i, 1)
    o_scr[rows, lo:lo + dk] = o3.reshape(n_rows, dk)


def _hgrn_operands(act_scr, rows, h, chunk):
    dk = HGRN_HEAD_DIM
    vdim = HGRN_HEADS * dk
    nb = chunk // SUBLANES
    lo = h * dk
    q = act_scr[rows, lo:lo + dk]
    fg = act_scr[rows, vdim + lo:vdim + lo + dk]
    lf = act_scr[rows, 2 * vdim + lo:2 * vdim + lo + dk]
    v = act_scr[rows, 3 * vdim + lo:3 * vdim + lo + dk]
    k = 1.0 - fg

    shape3 = (nb, SUBLANES, dk)
    q3, k3, v3 = q.reshape(shape3), k.reshape(shape3), v.reshape(shape3)
    ridx = lax.broadcasted_iota(jnp.int32, shape3, 1)

    w3 = lf.reshape(shape3)
    for s in (1, 2, 4):
        w3 = w3 + jnp.where(ridx >= s, pltpu.roll(w3, s, 1), 0.0)
    tot = w3[:, SUBLANES - 1:SUBLANES, :]
    offs = [jnp.zeros((1, 1, dk), F32)]
    for j in range(nb):
        offs.append(offs[-1] + tot[j:j + 1])

    q_blk = q3 * jnp.exp2(w3)
    k_blk = k3 * jnp.exp2(tot - w3)
    if nb > 1:
        q_in = q_blk * jnp.exp2(jnp.concatenate(offs[:nb], axis=0))
        k_end = k_blk * jnp.exp2(jnp.concatenate([offs[nb] - offs[j + 1] for j in range(nb)], axis=0))
    else:
        q_in, k_end = q_blk, k_blk

    k_st = v_st = far_mask = None
    if nb > 1:
        k_rows, v_rows, starts = [], [], []
        n_rows = 0
        for i in range(1, nb):
            starts.append(n_rows)
            for j in range(i):
                k_rows.append(k_blk[j] * jnp.exp2(offs[i][0] - offs[j + 1][0]))
                v_rows.append(v3[j])
                n_rows += SUBLANES
        pad = -n_rows % BF16_ROWS
        if pad:
            k_rows.append(jnp.zeros((pad, dk), F32))
            v_rows.append(jnp.zeros((pad, dk), F32))
        k_st = jnp.concatenate(k_rows, axis=0).astype(BF16)
        v_st = jnp.concatenate(v_rows, axis=0).astype(BF16)
        r_tot = n_rows + pad
        row_blk = lax.broadcasted_iota(jnp.int32, (chunk, r_tot), 0) // SUBLANES
        col = lax.broadcasted_iota(jnp.int32, (chunk, r_tot), 1)
        col_blk = jnp.where(col < n_rows, 1, 0)
        for s0 in starts[1:]:
            col_blk = col_blk + jnp.where(col >= s0, 1, 0)
        far_mask = row_blk == col_blk

    e_tot = jnp.exp2(offs[nb][0])
    e_hi = e_tot.astype(BF16).astype(F32)
    e_mid = (e_tot - e_hi).astype(BF16).astype(F32)
    e_lo = e_tot - e_hi - e_mid
    prow = lax.broadcasted_iota(jnp.int32, (BF16_ROWS, dk), 0)
    pieces = jnp.where(prow == 0, e_hi, jnp.where(prow == 1, e_mid, jnp.where(prow == 2, e_lo, 0.0)))
    lhs_state = jnp.concatenate([k_end.reshape(chunk, dk), pieces], axis=0).astype(BF16)
    v_pad = jnp.concatenate([v, jnp.zeros((BF16_ROWS, dk), F32)], axis=0).astype(BF16)
    return (q_in.reshape(chunk, dk).astype(BF16), q_blk.reshape(chunk, dk).astype(BF16),
            k_st, v_st, far_mask, lhs_state, v_pad)


def _hgrn_chunk(act_scr, yg_scr, state_refs, rows, hn_row, ones_bf, ones_rows_bf, chunk):
    dk = HGRN_HEAD_DIM
    vdim = HGRN_HEADS * dk

    def finish(h, o, p, v_st):
        o = o + yg_scr[rows, h * dk:(h + 1) * dk]
        if p is not None:
            o = o + _dot(p, v_st)
        ms = jnp.mean(o * o, axis=-1, keepdims=True)
        gs = act_scr[rows, 4 * vdim + h * dk:4 * vdim + (h + 1) * dk]
        yg_scr[rows, h * dk:(h + 1) * dk] = ((o * lax.rsqrt(ms + NORM_EPS)) * hn_row[:, h * dk:(h + 1) * dk]) * gs

    all_heads = list(range(HGRN_HEADS))
    if chunk > SUBLANES:
        first = {}
        for g0 in range(0, HGRN_HEADS, HGRN_HEAD_GROUP):
            group = all_heads[g0:g0 + HGRN_HEAD_GROUP]
            first.update(_hgrn_first_matmuls(act_scr, state_refs, rows, ones_rows_bf, chunk, group))
            for h in group:
                _hgrn_near_pairs(act_scr, yg_scr, rows, h, ones_bf)
        for h in all_heads:
            finish(h, *first[h])
    else:
        first = {}
        for h in all_heads:
            _hgrn_near_pairs(act_scr, yg_scr, rows, h, ones_bf)
            first.update(_hgrn_first_matmuls(act_scr, state_refs, rows, ones_rows_bf, chunk, [h]))
        for h in all_heads:
            finish(h, *first[h])


def _hgrn_first_matmuls(act_scr, state_refs, rows, ones_rows_bf, chunk, heads):
    dk = HGRN_HEAD_DIM
    ops = {h: _hgrn_operands(act_scr, rows, h, chunk) for h in heads}
    st = {h: state_refs[h][0][...] for h in heads}
    o = {h: _dot(ops[h][0], st[h].astype(BF16)) for h in heads}
    p = {h: None for h in heads}
    if chunk > SUBLANES:
        p = {h: jnp.where(ops[h][4], _dot_nt(ops[h][1], ops[h][2]), 0.0).astype(BF16) for h in heads}
    for h in heads:
        res = _dot_tn(ops[h][5], jnp.concatenate([ops[h][6], ones_rows_bf], axis=1))
        state_refs[h][1][...] = res[:, dk:] * st[h] + res[:, :dk]
    return {h: (o[h], p[h], ops[h][3]) for h in heads}


def _hgrn_normed(x_ref, gmix_ref):
    return _rms(x_ref[...], gmix_ref[...]).astype(BF16)


def _hgrn_project_cols(xn, win_ref, lb_row, act, c0):
    vdim = HGRN_HEADS * HGRN_HEAD_DIM
    blk = _dot(xn, win_ref[:, c0:c0 + PROJ_COLS])
    sect, s0 = divmod(c0, vdim)
    if sect == 0:
        act[:, s0:s0 + PROJ_COLS] = blk * jax.nn.sigmoid(blk)
    elif sect == 1:
        lb = lb_row[:, s0:s0 + PROJ_COLS]
        fgate = lb + (1.0 - lb) * jax.nn.sigmoid(blk)
        act[:, vdim + s0:vdim + s0 + PROJ_COLS] = fgate
        act[:, 2 * vdim + s0:2 * vdim + s0 + PROJ_COLS] = jnp.log2(fgate)
    elif sect == 2:
        act[:, 3 * vdim + s0:3 * vdim + s0 + PROJ_COLS] = blk
    else:
        act[:, 4 * vdim + s0:4 * vdim + s0 + PROJ_COLS] = jax.nn.sigmoid(blk)


def _hgrn_kernel(x_ref, xnext_ref, s0_ref, gmix_ref, win_ref, lbp_ref, hn_ref, wout_ref, y_ref, sout_ref,
                 act_scr, yg_scr, st_scr=None, *, bb, tb, chunk, layer_slot):
    t = pl.program_id(1)
    step = pl.program_id(0) * pl.num_programs(1) + t
    slot = lax.rem(step, 2)
    d = x_ref.shape[-1]
    m = bb * tb
    dk = HGRN_HEAD_DIM
    single_time_block = st_scr is None

    if not single_time_block:
        @pl.when(t == 0)
        def _():
            st_scr[...] = s0_ref[...]

    lbp = lbp_ref[...]
    e = jnp.exp(lbp - jnp.max(lbp, axis=0, keepdims=True))
    sm = e / jnp.sum(e, axis=0, keepdims=True)
    lb_row = sm[0:1]
    for s in range(1, layer_slot + 1):
        lb_row = lb_row + sm[s:s + 1]

    proj_cols = list(range(0, 4 * HGRN_HEADS * dk, PROJ_COLS))

    @pl.when(step == 0)
    def _():
        xn0 = _hgrn_normed(x_ref, gmix_ref)
        for c0 in proj_cols:
            _hgrn_project_cols(xn0, win_ref, lb_row, act_scr.at[0], c0)

    def run(act, act_next):
        xn_next = _hgrn_normed(xnext_ref, gmix_ref)
        hn_row = hn_ref[...]
        orow = lax.broadcasted_iota(jnp.int32, (chunk + BF16_ROWS, dk), 0)
        ones_rows_bf = jnp.where((orow >= chunk) & (orow < chunk + 3), 1.0, 0.0).astype(BF16)
        chunks_per_b = tb // chunk
        n_chunks = m // chunk
        for ci in range(n_chunks):
            for c0 in proj_cols[ci * len(proj_cols) // n_chunks:(ci + 1) * len(proj_cols) // n_chunks]:
                _hgrn_project_cols(xn_next, win_ref, lb_row, act_next, c0)
            bi = ci // chunks_per_b
            if single_time_block:
                state_refs = [(s0_ref.at[bi, h], sout_ref.at[bi, h]) for h in range(HGRN_HEADS)]
            else:
                state_refs = [(st_scr.at[bi, h], st_scr.at[bi, h]) for h in range(HGRN_HEADS)]
            _hgrn_chunk(act, yg_scr, state_refs, slice(ci * chunk, (ci + 1) * chunk), hn_row,
                        jnp.ones((dk, dk), BF16), ones_rows_bf, chunk)
        y = _dot(yg_scr[...].astype(BF16), wout_ref[...])
        y_ref[...] = x_ref[...] + y

    for parity in (0, 1):
        @pl.when(slot == parity)
        def _():
            run(act_scr.at[parity], act_scr.at[1 - parity])

    if not single_time_block:
        @pl.when(t == pl.num_programs(1) - 1)
        def _():
            sout_ref[...] = st_scr[...]


def _hgrn(x, b, t, s0, gmix, win, lb_params, hnorm, wout, layer, *, layer_slot, bb, tb):
    d = x.shape[1]
    chunk = min(HGRN_MAX_CHUNK, t)
    assert t % tb == 0 and tb % chunk == 0 and b % bb == 0 and chunk % SUBLANES == 0
    heads, dk = HGRN_HEADS, HGRN_HEAD_DIM
    vdim = heads * dk
    assert win.shape[2] == 4 * vdim and vdim % PROJ_COLS == 0
    m = bb * tb
    nj = t // tb
    n_steps = (b // bb) * nj

    assert bb == 1 or tb == t
    this_block = lambda i, j: (i * nj + j, 0)
    next_block = lambda i, j: (jnp.minimum(i * nj + j + 1, n_steps - 1), 0)

    scratch = [pltpu.VMEM((2, m, 5 * vdim), F32), pltpu.VMEM((m, vdim), F32)]
    if nj > 1:
        scratch.append(pltpu.VMEM((bb, heads, dk, dk), F32))
    return pl.pallas_call(
        functools.partial(_hgrn_kernel, bb=bb, tb=tb, chunk=chunk, layer_slot=layer_slot),
        grid=(b // bb, nj),
        in_specs=[
            pl.BlockSpec((m, d), this_block),
            pl.BlockSpec((m, d), next_block),
            pl.BlockSpec((None, bb, heads, dk, dk), lambda i, j: (layer_slot, i, 0, 0, 0)),
            _layer_spec(gmix, layer, 2),
            _layer_spec(win, layer_slot, 2),
            pl.BlockSpec(lb_params.shape, lambda i, j: (0, 0)),
            _layer_spec(hnorm, layer_slot, 2),
            _layer_spec(wout, layer_slot, 2),
        ],
        out_specs=[
            pl.BlockSpec((m, d), this_block),
            pl.BlockSpec((bb, heads, dk, dk), lambda i, j: (i, 0, 0, 0)),
        ],
        out_shape=[
            jax.ShapeDtypeStruct((b * t, d), F32),
            jax.ShapeDtypeStruct((b, heads, dk, dk), F32),
        ],
        scratch_shapes=scratch,
        compiler_params=pltpu.CompilerParams(
            dimension_semantics=("arbitrary", "arbitrary"), vmem_limit_bytes=VMEM_LIMIT_BYTES),
        name="hgrn_mixer",
    )(x, x, s0, gmix, win, lb_params, hnorm, wout)


def _conv_kernel(x_ref, buf_ref, gmix_ref, win_ref, cw_ref, wout_ref, y_ref, bufo_ref, ext_scr, *, bb, tb):
    t = pl.program_id(1)
    d = x_ref.shape[-1]
    m = bb * tb
    halo = SUBLANES
    tail = CONV_W - 1

    @pl.when(t == 0)
    def _():
        ext_scr[:, halo - tail:halo, :] = buf_ref[...]

    x = x_ref[...].reshape(m, d)
    xn = _rms(x, gmix_ref[...]).astype(BF16)
    b_gate = _dot(xn, win_ref[:, 0:d].astype(BF16))
    c_gate = _dot(xn, win_ref[:, d:2 * d].astype(BF16))
    v = _dot(xn, win_ref[:, 2 * d:3 * d].astype(BF16))
    ext_scr[:, halo:halo + tb, :] = (c_gate * v).reshape(bb, tb, d)

    cw = cw_ref[...]
    conv = None
    for tap in range(CONV_W):
        start = halo - tail + tap
        term = cw[tap:tap + 1] * ext_scr[:, start:start + tb, :]
        conv = term if conv is None else conv + term
    y = _dot((b_gate * conv.reshape(m, d)).astype(BF16), wout_ref[...].astype(BF16))
    y_ref[...] = (x + y).reshape(bb, tb, d)

    new_tail = ext_scr[:, halo + tb - tail:halo + tb, :]
    ext_scr[:, halo - tail:halo, :] = new_tail

    @pl.when(t == pl.num_programs(1) - 1)
    def _():
        bufo_ref[...] = new_tail


def _conv(x, buf, gmix, win, cw, wout, layer, *, layer_slot, bb, tb):
    b, t, d = x.shape
    tail = CONV_W - 1
    assert t % tb == 0 and b % bb == 0 and tb >= tail
    return pl.pallas_call(
        functools.partial(_conv_kernel, bb=bb, tb=tb),
        grid=(b // bb, t // tb),
        in_specs=[
            pl.BlockSpec((bb, tb, d), lambda i, j: (i, j, 0)),
            pl.BlockSpec((None, bb, tail, d), lambda i, j: (layer_slot, i, 0, 0)),
            _layer_spec(gmix, layer, 2),
            _layer_spec(win, layer_slot, 2),
            _layer_spec(cw, layer_slot, 2),
            _layer_spec(wout, layer_slot, 2),
        ],
        out_specs=[
            pl.BlockSpec((bb, tb, d), lambda i, j: (i, j, 0)),
            pl.BlockSpec((bb, tail, d), lambda i, j: (i, 0, 0)),
        ],
        out_shape=[
            jax.ShapeDtypeStruct((b, t, d), F32),
            jax.ShapeDtypeStruct((b, tail, d), F32),
        ],
        scratch_shapes=[pltpu.VMEM((bb, tb + SUBLANES, d), F32)],
        compiler_params=pltpu.CompilerParams(
            dimension_semantics=("arbitrary", "arbitrary"), vmem_limit_bytes=VMEM_LIMIT_BYTES),
        name="conv_mixer",
    )(x, buf, gmix, win, cw, wout)


def kernel(x_prompt, x_sample, state_hgrn, state_conv, norm_ffn1, w_ffn1_up, w_ffn1_down, norm_mix, norm_ffn2, w_ffn2_up, w_ffn2_down, w_hgrn_in, hgrn_lower_bounds, hgrn_norm, w_hgrn_out, w_conv_in, conv_w, w_conv_out, norm_final):
    g_ffn1, g_mix, g_ffn2, g_hgrn = _rows(norm_ffn1), _rows(norm_mix), _rows(norm_ffn2), _rows(hgrn_norm)
    w1_up, w1_dn, w2_up, w2_dn = w_ffn1_up, w_ffn1_down, w_ffn2_up, w_ffn2_down
    wc_in, wc_out = w_conv_in, w_conv_out
    wh_in, wh_out = w_hgrn_in.astype(BF16), w_hgrn_out.astype(BF16)
    lb_params = hgrn_lower_bounds.astype(F32)

    (bp, tp, d), (bs, ts, _) = x_prompt.shape, x_sample.shape
    n_hgrn, _, heads, dk, dv = state_hgrn.shape
    n_conv, _, tail, _ = state_conv.shape
    depth = norm_ffn1.shape[0]
    zero_hgrn = jnp.zeros((n_hgrn, bp, heads, dk, dv), x_prompt.dtype)
    zero_conv = jnp.zeros((n_conv, bp, tail, d), x_prompt.dtype)

    xp, xs = x_prompt.reshape(bp * tp, d), x_sample.reshape(bs * ts, d)
    hgrn_p, hgrn_s, conv_p, conv_s = [], [], [], []
    for layer in range(depth):
        xp, xs = _ffn(xp, xs, g_ffn1, w1_up, w1_dn, norm_final, layer, final_norm=False, tm=FFN_ROWS)
        j = layer // 2
        if layer % 2 == 0:
            xp, s = _hgrn(xp, bp, tp, zero_hgrn, g_mix, wh_in, lb_params, g_hgrn, wh_out, layer,
                          layer_slot=j, bb=1, tb=HGRN_PROMPT_ROWS)
            hgrn_p.append(s)
            xs, s = _hgrn(xs, bs, ts, state_hgrn, g_mix, wh_in, lb_params, g_hgrn, wh_out, layer,
                          layer_slot=j, bb=HGRN_SAMPLE_SEQS, tb=ts)
            hgrn_s.append(s)
        else:
            x3, s = _conv(xp.reshape(bp, tp, d), zero_conv, g_mix, wc_in, conv_w, wc_out, layer,
                          layer_slot=j, bb=1, tb=CONV_PROMPT_ROWS)
            xp = x3.reshape(bp * tp, d)
            conv_p.append(s)
            x3, s = _conv(xs.reshape(bs, ts, d), state_conv, g_mix, wc_in, conv_w, wc_out, layer,
                          layer_slot=j, bb=CONV_SAMPLE_SEQS, tb=ts)
            xs = x3.reshape(bs * ts, d)
            conv_s.append(s)
        xp, xs = _ffn(xp, xs, g_ffn2, w2_up, w2_dn, norm_final, layer,
                      final_norm=(layer == depth - 1), tm=FFN_ROWS)
    return (xp.reshape(bp, tp, d), xs.reshape(bs, ts, d), jnp.stack(hgrn_p), jnp.stack(hgrn_s),
            jnp.stack(conv_p), jnp.stack(conv_s))
```

```python
import functools

import jax
import jax.numpy as jnp
from jax import lax
from jax.experimental import pallas as pl
from jax.experimental.pallas import tpu as pltpu

F32 = jnp.float32
BF16 = jnp.bfloat16

NORM_EPS = 1e-6
FFN_RES = 0.5
HGRN_HEADS = 8
HGRN_HEAD_DIM = 128
HGRN_MAX_CHUNK = 32
HGRN_HEAD_GROUP = 8
NEAR_FIRST_MATMUL_SHIFT = 8
CONV_W = 3
SUBLANES = 8
BF16_ROWS = 16
LANES = 128
FFN_CHUNK = 256
PROJ_COLS = 512
VMEM_LIMIT_BYTES = 58 * 1024 * 1024
FFN_ROWS = 512
HGRN_PROMPT_ROWS = 512
HGRN_SAMPLE_SEQS = 16
CONV_PROMPT_ROWS = 512
CONV_SAMPLE_SEQS = 16


def _rms(x, gain):
    r = lax.rsqrt(jnp.mean(x * x, axis=-1, keepdims=True) + NORM_EPS)
    return (x * r) * gain


def _dot(a, b):
    return jnp.dot(a, b, preferred_element_type=F32)


def _dot_nt(a, b):
    return lax.dot_general(a, b, (((1,), (1,)), ((), ())), preferred_element_type=F32)


def _dot_tn(a, b):
    return lax.dot_general(a, b, (((0,), (0,)), ((), ())), preferred_element_type=F32)


def _layer_spec(w, layer, n_grid):
    zeros = (0,) * (w.ndim - 1)
    index_map = (lambda i: (layer,) + zeros) if n_grid == 1 else (lambda i, j: (layer,) + zeros)
    return pl.BlockSpec((None,) + w.shape[1:], index_map, pipeline_mode=pl.Buffered(1))


def _rows(p):
    return p.reshape(p.shape[0], 1, p.shape[1])


def _ffn_kernel(xa_ref, xb_ref, g_ref, wup_ref, wdn_ref, gf_ref, oa_ref, ob_ref, *, n_a, final_norm):
    ff = wdn_ref.shape[0]

    def tile(x_ref, o_ref):
        x = x_ref[...]
        xn = _rms(x, g_ref[...]).astype(BF16)
        acc = jnp.zeros_like(x)
        for c0 in range(0, ff, FFN_CHUNK):
            gate = _dot(xn, wup_ref[:, c0:c0 + FFN_CHUNK].astype(BF16))
            up = _dot(xn, wup_ref[:, ff + c0:ff + c0 + FFN_CHUNK].astype(BF16))
            act = (gate * jax.nn.sigmoid(gate) * up).astype(BF16)
            acc = acc + _dot(act, wdn_ref[c0:c0 + FFN_CHUNK, :].astype(BF16))
        out = x + FFN_RES * acc
        if final_norm:
            out = _rms(out, gf_ref[...])
        o_ref[...] = out

    on_a = pl.program_id(0) < n_a
    pl.when(on_a)(lambda: tile(xa_ref, oa_ref))
    pl.when(jnp.logical_not(on_a))(lambda: tile(xb_ref, ob_ref))


def _ffn(xa, xb, gains, wup, wdn, gain_final, layer, *, final_norm, tm):
    (na, d), nb = xa.shape, xb.shape[0]
    ff = wdn.shape[1]
    assert na % tm == 0 and nb % tm == 0 and ff % FFN_CHUNK == 0 and wup.shape[2] == 2 * ff
    n_a, n_b = na // tm, nb // tm
    a_block = lambda i: (jnp.minimum(i, n_a - 1), 0)
    b_block = lambda i: (jnp.maximum(i - n_a, 0), 0)
    return pl.pallas_call(
        functools.partial(_ffn_kernel, n_a=n_a, final_norm=final_norm),
        grid=(n_a + n_b,),
        in_specs=[
            pl.BlockSpec((tm, d), a_block),
            pl.BlockSpec((tm, d), b_block),
            _layer_spec(gains, layer, 1),
            _layer_spec(wup, layer, 1),
            _layer_spec(wdn, layer, 1),
            pl.BlockSpec((1, d), lambda i: (0, 0)),
        ],
        out_specs=[pl.BlockSpec((tm, d), a_block), pl.BlockSpec((tm, d), b_block)],
        out_shape=[jax.ShapeDtypeStruct((na, d), F32), jax.ShapeDtypeStruct((nb, d), F32)],
        compiler_params=pltpu.CompilerParams(
            dimension_semantics=("arbitrary",), vmem_limit_bytes=VMEM_LIMIT_BYTES),
        name="ffn_final" if final_norm else "ffn",
    )(xa, xb, gains, wup, wdn, gain_final.reshape(1, d))


def _hgrn_near_pairs(act_scr, o_scr, rows, h, ones_bf):
    dk = HGRN_HEAD_DIM
    vdim = HGRN_HEADS * dk
    lo = h * dk
    q = act_scr[rows, lo:lo + dk]
    fg = act_scr[rows, vdim + lo:vdim + lo + dk]
    v = act_scr[rows, 3 * vdim + lo:3 * vdim + lo + dk]
    n_rows = q.shape[0]
    shape3 = (n_rows // SUBLANES, SUBLANES, dk)
    q3, fg3, v3 = q.reshape(shape3), fg.reshape(shape3), v.reshape(shape3)
    k3 = 1.0 - fg3
    ridx = lax.broadcasted_iota(jnp.int32, shape3, 1)
    fm = jnp.where(ridx == 0, 0.0, fg3)
    decay = fm
    o3 = jnp.sum(q3 * k3, axis=-1, keepdims=True) * v3
    matmul_terms = []
    for d in range(1, SUBLANES):
        if d > 1:
            decay = decay * pltpu.roll(fm, d - 1, 1)
        term = (q3 * decay) * pltpu.roll(k3, d, 1)
        if d < NEAR_FIRST_MATMUL_SHIFT:
            o3 = o3 + jnp.sum(term, axis=-1, keepdims=True) * pltpu.roll(v3, d, 1)
        else:
            matmul_terms.append(term)
    if matmul_terms:
        stacked = jnp.concatenate(matmul_terms, axis=0).reshape(len(matmul_terms) * n_rows, dk)
        sums = _dot(stacked.astype(BF16), ones_bf).reshape((len(matmul_terms),) + shape3)
        for i in range(len(matmul_terms)):
            o3 = o3 + sums[i] * pltpu.roll(v3, NEAR_FIRST_MATMUL_SHIFT + i, 1)
    o_scr[rows, lo:lo + dk] = o3.reshape(n_rows, dk)


def _hgrn_operands(act_scr, rows, h, chunk):
    dk = HGRN_HEAD_DIM
    vdim = HGRN_HEADS * dk
    nb = chunk // SUBLANES
    lo = h * dk
    q = act_scr[rows, lo:lo + dk]
    fg = act_scr[rows, vdim + lo:vdim + lo + dk]
    lf = act_scr[rows, 2 * vdim + lo:2 * vdim + lo + dk]
    v = act_scr[rows, 3 * vdim + lo:3 * vdim + lo + dk]
    k = 1.0 - fg

    shape3 = (nb, SUBLANES, dk)
    q3, k3, v3 = q.reshape(shape3), k.reshape(shape3), v.reshape(shape3)
    ridx = lax.broadcasted_iota(jnp.int32, shape3, 1)

    w3 = lf.reshape(shape3)
    for s in (1, 2, 4):
        w3 = w3 + jnp.where(ridx >= s, pltpu.roll(w3, s, 1), 0.0)
    tot = w3[:, SUBLANES - 1:SUBLANES, :]
    offs = [jnp.zeros((1, 1, dk), F32)]
    for j in range(nb):
        offs.append(offs[-1] + tot[j:j + 1])

    q_blk = q3 * jnp.exp2(w3)
    k_blk = k3 * jnp.exp2(tot - w3)
    if nb > 1:
        q_in = q_blk * jnp.exp2(jnp.concatenate(offs[:nb], axis=0))
        k_end = k_blk * jnp.exp2(jnp.concatenate([offs[nb] - offs[j + 1] for j in range(nb)], axis=0))
    else:
        q_in, k_end = q_blk, k_blk

    k_st = v_st = far_mask = None
    if nb > 1:
        k_rows, v_rows, starts = [], [], []
        n_rows = 0
        for i in range(1, nb):
            starts.append(n_rows)
            for j in range(i):
                k_rows.append(k_blk[j] * jnp.exp2(offs[i][0] - offs[j + 1][0]))
                v_rows.append(v3[j])
                n_rows += SUBLANES
        pad = -n_rows % BF16_ROWS
        if pad:
            k_rows.append(jnp.zeros((pad, dk), F32))
            v_rows.append(jnp.zeros((pad, dk), F32))
        k_st = jnp.concatenate(k_rows, axis=0).astype(BF16)
        v_st = jnp.concatenate(v_rows, axis=0).astype(BF16)
        r_tot = n_rows + pad
        row_blk = lax.broadcasted_iota(jnp.int32, (chunk, r_tot), 0) // SUBLANES
        col = lax.broadcasted_iota(jnp.int32, (chunk, r_tot), 1)
        col_blk = jnp.where(col < n_rows, 1, 0)
        for s0 in starts[1:]:
            col_blk = col_blk + jnp.where(col >= s0, 1, 0)
        far_mask = row_blk == col_blk

    e_tot = jnp.exp2(offs[nb][0])
    e_hi = e_tot.astype(BF16).astype(F32)
    e_mid = (e_tot - e_hi).astype(BF16).astype(F32)
    e_lo = e_tot - e_hi - e_mid
    prow = lax.broadcasted_iota(jnp.int32, (BF16_ROWS, dk), 0)
    pieces = jnp.where(prow == 0, e_hi, jnp.where(prow == 1, e_mid, jnp.where(prow == 2, e_lo, 0.0)))
    lhs_state = jnp.concatenate([k_end.reshape(chunk, dk), pieces], axis=0).astype(BF16)
    v_pad = jnp.concatenate([v, jnp.zeros((BF16_ROWS, dk), F32)], axis=0).astype(BF16)
    return (q_in.reshape(chunk, dk).astype(BF16), q_blk.reshape(chunk, dk).astype(BF16),
            k_st, v_st, far_mask, lhs_state, v_pad)


def _hgrn_chunk(act_scr, yg_scr, state_refs, rows, hn_row, ones_bf, ones_rows_bf, chunk):
    dk = HGRN_HEAD_DIM
    vdim = HGRN_HEADS * dk

    def finish(h, o, p, v_st):
        o = o + yg_scr[rows, h * dk:(h + 1) * dk]
        if p is not None:
            o = o + _dot(p, v_st)
        ms = jnp.mean(o * o, axis=-1, keepdims=True)
        gs = act_scr[rows, 4 * vdim + h * dk:4 * vdim + (h + 1) * dk]
        yg_scr[rows, h * dk:(h + 1) * dk] = ((o * lax.rsqrt(ms + NORM_EPS)) * hn_row[:, h * dk:(h + 1) * dk]) * gs

    all_heads = list(range(HGRN_HEADS))
    if chunk > SUBLANES:
        first = {}
        for g0 in range(0, HGRN_HEADS, HGRN_HEAD_GROUP):
            group = all_heads[g0:g0 + HGRN_HEAD_GROUP]
            first.update(_hgrn_first_matmuls(act_scr, state_refs, rows, ones_rows_bf, chunk, group))
            for h in group:
                _hgrn_near_pairs(act_scr, yg_scr, rows, h, ones_bf)
        for h in all_heads:
            finish(h, *first[h])
    else:
        first = {}
        for h in all_heads:
            _hgrn_near_pairs(act_scr, yg_scr, rows, h, ones_bf)
            first.update(_hgrn_first_matmuls(act_scr, state_refs, rows, ones_rows_bf, chunk, [h]))
        for h in all_heads:
            finish(h, *first[h])


def _hgrn_first_matmuls(act_scr, state_refs, rows, ones_rows_bf, chunk, heads):
    dk = HGRN_HEAD_DIM
    ops = {h: _hgrn_operands(act_scr, rows, h, chunk) for h in heads}
    st = {h: state_refs[h][0][...] for h in heads}
    o = {h: _dot(ops[h][0], st[h].astype(BF16)) for h in heads}
    p = {h: None for h in heads}
    if chunk > SUBLANES:
        p = {h: jnp.where(ops[h][4], _dot_nt(ops[h][1], ops[h][2]), 0.0).astype(BF16) for h in heads}
    for h in heads:
        res = _dot_tn(ops[h][5], jnp.concatenate([ops[h][6], ones_rows_bf], axis=1))
        state_refs[h][1][...] = res[:, dk:] * st[h] + res[:, :dk]
    return {h: (o[h], p[h], ops[h][3]) for h in heads}


def _hgrn_normed(x_ref, gmix_ref):
    return _rms(x_ref[...], gmix_ref[...]).astype(BF16)


def _hgrn_project_cols(xn, win_ref, lb_row, act, c0):
    vdim = HGRN_HEADS * HGRN_HEAD_DIM
    blk = _dot(xn, win_ref[:, c0:c0 + PROJ_COLS])
    sect, s0 = divmod(c0, vdim)
    if sect == 0:
        act[:, s0:s0 + PROJ_COLS] = blk * jax.nn.sigmoid(blk)
    elif sect == 1:
        lb = lb_row[:, s0:s0 + PROJ_COLS]
        fgate = lb + (1.0 - lb) * jax.nn.sigmoid(blk)
        act[:, vdim + s0:vdim + s0 + PROJ_COLS] = fgate
        act[:, 2 * vdim + s0:2 * vdim + s0 + PROJ_COLS] = jnp.log2(fgate)
    elif sect == 2:
        act[:, 3 * vdim + s0:3 * vdim + s0 + PROJ_COLS] = blk
    else:
        act[:, 4 * vdim + s0:4 * vdim + s0 + PROJ_COLS] = jax.nn.sigmoid(blk)


def _hgrn_kernel(x_ref, xnext_ref, s0_ref, gmix_ref, win_ref, lbp_ref, hn_ref, wout_ref, y_ref, sout_ref,
                 act_scr, yg_scr, st_scr=None, *, bb, tb, chunk, layer_slot):
    t = pl.program_id(1)
    step = pl.program_id(0) * pl.num_programs(1) + t
    slot = lax.rem(step, 2)
    d = x_ref.shape[-1]
    m = bb * tb
    dk = HGRN_HEAD_DIM
    single_time_block = st_scr is None

    if not single_time_block:
        @pl.when(t == 0)
        def _():
            st_scr[...] = s0_ref[...]

    lbp = lbp_ref[...]
    e = jnp.exp(lbp - jnp.max(lbp, axis=0, keepdims=True))
    sm = e / jnp.sum(e, axis=0, keepdims=True)
    lb_row = sm[0:1]
    for s in range(1, layer_slot + 1):
        lb_row = lb_row + sm[s:s + 1]

    proj_cols = list(range(0, 4 * HGRN_HEADS * dk, PROJ_COLS))

    @pl.when(step == 0)
    def _():
        xn0 = _hgrn_normed(x_ref, gmix_ref)
        for c0 in proj_cols:
            _hgrn_project_cols(xn0, win_ref, lb_row, act_scr.at[0], c0)

    def run(act, act_next):
        xn_next = _hgrn_normed(xnext_ref, gmix_ref)
        hn_row = hn_ref[...]
        orow = lax.broadcasted_iota(jnp.int32, (chunk + BF16_ROWS, dk), 0)
        ones_rows_bf = jnp.where((orow >= chunk) & (orow < chunk + 3), 1.0, 0.0).astype(BF16)
        chunks_per_b = tb // chunk
        n_chunks = m // chunk
        for ci in range(n_chunks):
            for c0 in proj_cols[ci * len(proj_cols) // n_chunks:(ci + 1) * len(proj_cols) // n_chunks]:
                _hgrn_project_cols(xn_next, win_ref, lb_row, act_next, c0)
            bi = ci // chunks_per_b
            if single_time_block:
                state_refs = [(s0_ref.at[bi, h], sout_ref.at[bi, h]) for h in range(HGRN_HEADS)]
            else:
                state_refs = [(st_scr.at[bi, h], st_scr.at[bi, h]) for h in range(HGRN_HEADS)]
            _hgrn_chunk(act, yg_scr, state_refs, slice(ci * chunk, (ci + 1) * chunk), hn_row,
                        jnp.ones((dk, dk), BF16), ones_rows_bf, chunk)
        y = _dot(yg_scr[...].astype(BF16), wout_ref[...])
        y_ref[...] = x_ref[...] + y

    for parity in (0, 1):
        @pl.when(slot == parity)
        def _():
            run(act_scr.at[parity], act_scr.at[1 - parity])

    if not single_time_block:
        @pl.when(t == pl.num_programs(1) - 1)
        def _():
            sout_ref[...] = st_scr[...]


def _hgrn(x, b, t, s0, gmix, win, lb_params, hnorm, wout, layer, *, layer_slot, bb, tb):
    d = x.shape[1]
    chunk = min(HGRN_MAX_CHUNK, t)
    assert t % tb == 0 and tb % chunk == 0 and b % bb == 0 and chunk % SUBLANES == 0
    heads, dk = HGRN_HEADS, HGRN_HEAD_DIM
    vdim = heads * dk
    assert win.shape[2] == 4 * vdim and vdim % PROJ_COLS == 0
    m = bb * tb
    nj = t // tb
    n_steps = (b // bb) * nj

    assert bb == 1 or tb == t
    this_block = lambda i, j: (i * nj + j, 0)
    next_block = lambda i, j: (jnp.minimum(i * nj + j + 1, n_steps - 1), 0)

    scratch = [pltpu.VMEM((2, m, 5 * vdim), F32), pltpu.VMEM((m, vdim), F32)]
    if nj > 1:
        scratch.append(pltpu.VMEM((bb, heads, dk, dk), F32))
    return pl.pallas_call(
        functools.partial(_hgrn_kernel, bb=bb, tb=tb, chunk=chunk, layer_slot=layer_slot),
        grid=(b // bb, nj),
        in_specs=[
            pl.BlockSpec((m, d), this_block),
            pl.BlockSpec((m, d), next_block),
            pl.BlockSpec((None, bb, heads, dk, dk), lambda i, j: (layer_slot, i, 0, 0, 0)),
            _layer_spec(gmix, layer, 2),
            _layer_spec(win, layer_slot, 2),
            pl.BlockSpec(lb_params.shape, lambda i, j: (0, 0)),
            _layer_spec(hnorm, layer_slot, 2),
            _layer_spec(wout, layer_slot, 2),
        ],
        out_specs=[
            pl.BlockSpec((m, d), this_block),
            pl.BlockSpec((bb, heads, dk, dk), lambda i, j: (i, 0, 0, 0)),
        ],
        out_shape=[
            jax.ShapeDtypeStruct((b * t, d), F32),
            jax.ShapeDtypeStruct((b, heads, dk, dk), F32),
        ],
        scratch_shapes=scratch,
        compiler_params=pltpu.CompilerParams(
            dimension_semantics=("arbitrary", "arbitrary"), vmem_limit_bytes=VMEM_LIMIT_BYTES),
        name="hgrn_mixer",
    )(x, x, s0, gmix, win, lb_params, hnorm, wout)


def _conv_kernel(x_ref, buf_ref, gmix_ref, win_ref, cw_ref, wout_ref, y_ref, bufo_ref, ext_scr, *, bb, tb):
    t = pl.program_id(1)
    d = x_ref.shape[-1]
    m = bb * tb
    halo = SUBLANES
    tail = CONV_W - 1

    @pl.when(t == 0)
    def _():
        ext_scr[:, halo - tail:halo, :] = buf_ref[...]

    x = x_ref[...].reshape(m, d)
    xn = _rms(x, gmix_ref[...]).astype(BF16)
    b_gate = _dot(xn, win_ref[:, 0:d].astype(BF16))
    c_gate = _dot(xn, win_ref[:, d:2 * d].astype(BF16))
    v = _dot(xn, win_ref[:, 2 * d:3 * d].astype(BF16))
    ext_scr[:, halo:halo + tb, :] = (c_gate * v).reshape(bb, tb, d)

    cw = cw_ref[...]
    conv = None
    for tap in range(CONV_W):
        start = halo - tail + tap
        term = cw[tap:tap + 1] * ext_scr[:, start:start + tb, :]
        conv = term if conv is None else conv + term
    y = _dot((b_gate * conv.reshape(m, d)).astype(BF16), wout_ref[...].astype(BF16))
    y_ref[...] = (x + y).reshape(bb, tb, d)

    new_tail = ext_scr[:, halo + tb - tail:halo + tb, :]
    ext_scr[:, halo - tail:halo, :] = new_tail

    @pl.when(t == pl.num_programs(1) - 1)
    def _():
        bufo_ref[...] = new_tail


def _conv(x, buf, gmix, win, cw, wout, layer, *, layer_slot, bb, tb):
    b, t, d = x.shape
    tail = CONV_W - 1
    assert t % tb == 0 and b % bb == 0 and tb >= tail
    return pl.pallas_call(
        functools.partial(_conv_kernel, bb=bb, tb=tb),
        grid=(b // bb, t // tb),
        in_specs=[
            pl.BlockSpec((bb, tb, d), lambda i, j: (i, j, 0)),
            pl.BlockSpec((None, bb, tail, d), lambda i, j: (layer_slot, i, 0, 0)),
            _layer_spec(gmix, layer, 2),
            _layer_spec(win, layer_slot, 2),
            _layer_spec(cw, layer_slot, 2),
            _layer_spec(wout, layer_slot, 2),
        ],
        out_specs=[
            pl.BlockSpec((bb, tb, d), lambda i, j: (i, j, 0)),
            pl.BlockSpec((bb, tail, d), lambda i, j: (i, 0, 0)),
        ],
        out_shape=[
            jax.ShapeDtypeStruct((b, t, d), F32),
            jax.ShapeDtypeStruct((b, tail, d), F32),
        ],
        scratch_shapes=[pltpu.VMEM((bb, tb + SUBLANES, d), F32)],
        compiler_params=pltpu.CompilerParams(
            dimension_semantics=("arbitrary", "arbitrary"), vmem_limit_bytes=VMEM_LIMIT_BYTES),
        name="conv_mixer",
    )(x, buf, gmix, win, cw, wout)


def kernel(x_prompt, x_sample, state_hgrn, state_conv, norm_ffn1, w_ffn1_up, w_ffn1_down, norm_mix, norm_ffn2, w_ffn2_up, w_ffn2_down, w_hgrn_in, hgrn_lower_bounds, hgrn_norm, w_hgrn_out, w_conv_in, conv_w, w_conv_out, norm_final):
    g_ffn1, g_mix, g_ffn2, g_hgrn = _rows(norm_ffn1), _rows(norm_mix), _rows(norm_ffn2), _rows(hgrn_norm)
    w1_up, w1_dn, w2_up, w2_dn = w_ffn1_up, w_ffn1_down, w_ffn2_up, w_ffn2_down
    wc_in, wc_out = w_conv_in, w_conv_out
    wh_in, wh_out = w_hgrn_in.astype(BF16), w_hgrn_out.astype(BF16)
    lb_params = hgrn_lower_bounds.astype(F32)

    (bp, tp, d), (bs, ts, _) = x_prompt.shape, x_sample.shape
    n_hgrn, _, heads, dk, dv = state_hgrn.shape
    n_conv, _, tail, _ = state_conv.shape
    depth = norm_ffn1.shape[0]
    zero_hgrn = jnp.zeros((n_hgrn, bp, heads, dk, dv), x_prompt.dtype)
    zero_conv = jnp.zeros((n_conv, bp, tail, d), x_prompt.dtype)

    xp, xs = x_prompt.reshape(bp * tp, d), x_sample.reshape(bs * ts, d)
    hgrn_p, hgrn_s, conv_p, conv_s = [], [], [], []
    for layer in range(depth):
        xp, xs = _ffn(xp, xs, g_ffn1, w1_up, w1_dn, norm_final, layer, final_norm=False, tm=FFN_ROWS)
        j = layer // 2
        if layer % 2 == 0:
            xp, s = _hgrn(xp, bp, tp, zero_hgrn, g_mix, wh_in, lb_params, g_hgrn, wh_out, layer,
                          layer_slot=j, bb=1, tb=HGRN_PROMPT_ROWS)
            hgrn_p.append(s)
            xs, s = _hgrn(xs, bs, ts, state_hgrn, g_mix, wh_in, lb_params, g_hgrn, wh_out, layer,
                          layer_slot=j, bb=HGRN_SAMPLE_SEQS, tb=ts)
            hgrn_s.append(s)
        else:
            x3, s = _conv(xp.reshape(bp, tp, d), zero_conv, g_mix, wc_in, conv_w, wc_out, layer,
                          layer_slot=j, bb=1, tb=CONV_PROMPT_ROWS)
            xp = x3.reshape(bp * tp, d)
            conv_p.append(s)
            x3, s = _conv(xs.reshape(bs, ts, d), state_conv, g_mix, wc_in, conv_w, wc_out, layer,
                          layer_slot=j, bb=CONV_SAMPLE_SEQS, tb=ts)
            xs = x3.reshape(bs * ts, d)
            conv_s.append(s)
        xp, xs = _ffn(xp, xs, g_ffn2, w2_up, w2_dn, norm_final, layer,
                      final_norm=(layer == depth - 1), tm=FFN_ROWS)
    return (xp.reshape(bp, tp, d), xs.reshape(bs, ts, d), jnp.stack(hgrn_p), jnp.stack(hgrn_s),
            jnp.stack(conv_p), jnp.stack(conv_s))
```

```python
import functools

import jax
import jax.numpy as jnp
from jax import lax
from jax.experimental import pallas as pl
from jax.experimental.pallas import tpu as pltpu

F32 = jnp.float32
BF16 = jnp.bfloat16

NORM_EPS = 1e-6
FFN_RES = 0.5
HGRN_HEADS = 8
HGRN_HEAD_DIM = 128
HGRN_MAX_CHUNK = 32
CONV_W = 3
SUBLANES = 8
BF16_ROWS = 16
LANES = 128
FFN_CHUNK = 256
PROJ_COLS = 512
VMEM_LIMIT_BYTES = 58 * 1024 * 1024
FFN_ROWS = 512
HGRN_PROMPT_ROWS = 512
HGRN_SAMPLE_SEQS = 16
CONV_PROMPT_ROWS = 512
CONV_SAMPLE_SEQS = 16


def _rms(x, gain):
    r = lax.rsqrt(jnp.mean(x * x, axis=-1, keepdims=True) + NORM_EPS)
    return (x * r) * gain


def _dot(a, b):
    return jnp.dot(a, b, preferred_element_type=F32)


def _dot_nt(a, b):
    return lax.dot_general(a, b, (((1,), (1,)), ((), ())), preferred_element_type=F32)


def _dot_tn(a, b):
    return lax.dot_general(a, b, (((0,), (0,)), ((), ())), preferred_element_type=F32)


def _layer_spec(w, layer, n_grid):
    zeros = (0,) * (w.ndim - 1)
    index_map = (lambda i: (layer,) + zeros) if n_grid == 1 else (lambda i, j: (layer,) + zeros)
    return pl.BlockSpec((None,) + w.shape[1:], index_map, pipeline_mode=pl.Buffered(1))


def _rows(p):
    return p.reshape(p.shape[0], 1, p.shape[1])


def _ffn_kernel(xa_ref, xb_ref, g_ref, wup_ref, wdn_ref, gf_ref, oa_ref, ob_ref, *, n_a, final_norm):
    ff = wdn_ref.shape[0]

    def tile(x_ref, o_ref):
        x = x_ref[...]
        xn = _rms(x, g_ref[...]).astype(BF16)
        acc = jnp.zeros_like(x)
        for c0 in range(0, ff, FFN_CHUNK):
            gate = _dot(xn, wup_ref[:, c0:c0 + FFN_CHUNK].astype(BF16))
            up = _dot(xn, wup_ref[:, ff + c0:ff + c0 + FFN_CHUNK].astype(BF16))
            act = (gate * jax.nn.sigmoid(gate) * up).astype(BF16)
            acc = acc + _dot(act, wdn_ref[c0:c0 + FFN_CHUNK, :].astype(BF16))
        out = x + FFN_RES * acc
        if final_norm:
            out = _rms(out, gf_ref[...])
        o_ref[...] = out

    on_a = pl.program_id(0) < n_a
    pl.when(on_a)(lambda: tile(xa_ref, oa_ref))
    pl.when(jnp.logical_not(on_a))(lambda: tile(xb_ref, ob_ref))


def _ffn(xa, xb, gains, wup, wdn, gain_final, layer, *, final_norm, tm):
    (na, d), nb = xa.shape, xb.shape[0]
    ff = wdn.shape[1]
    assert na % tm == 0 and nb % tm == 0 and ff % FFN_CHUNK == 0 and wup.shape[2] == 2 * ff
    n_a, n_b = na // tm, nb // tm
    a_block = lambda i: (jnp.minimum(i, n_a - 1), 0)
    b_block = lambda i: (jnp.maximum(i - n_a, 0), 0)
    return pl.pallas_call(
        functools.partial(_ffn_kernel, n_a=n_a, final_norm=final_norm),
        grid=(n_a + n_b,),
        in_specs=[
            pl.BlockSpec((tm, d), a_block),
            pl.BlockSpec((tm, d), b_block),
            _layer_spec(gains, layer, 1),
            _layer_spec(wup, layer, 1),
            _layer_spec(wdn, layer, 1),
            pl.BlockSpec((1, d), lambda i: (0, 0)),
        ],
        out_specs=[pl.BlockSpec((tm, d), a_block), pl.BlockSpec((tm, d), b_block)],
        out_shape=[jax.ShapeDtypeStruct((na, d), F32), jax.ShapeDtypeStruct((nb, d), F32)],
        compiler_params=pltpu.CompilerParams(
            dimension_semantics=("arbitrary",), vmem_limit_bytes=VMEM_LIMIT_BYTES),
        name="ffn_final" if final_norm else "ffn",
    )(xa, xb, gains, wup, wdn, gain_final.reshape(1, d))


def _hgrn_near_pairs(act_scr, o_scr, rows, h):
    dk = HGRN_HEAD_DIM
    vdim = HGRN_HEADS * dk
    lo = h * dk
    q = act_scr[rows, lo:lo + dk]
    fg = act_scr[rows, vdim + lo:vdim + lo + dk]
    v = act_scr[rows, 3 * vdim + lo:3 * vdim + lo + dk]
    n_rows = q.shape[0]
    shape3 = (n_rows // SUBLANES, SUBLANES, dk)
    q3, fg3, v3 = q.reshape(shape3), fg.reshape(shape3), v.reshape(shape3)
    k3 = 1.0 - fg3
    ridx = lax.broadcasted_iota(jnp.int32, shape3, 1)
    fm = jnp.where(ridx == 0, 0.0, fg3)
    g, v_shift = k3, v3
    o3 = jnp.sum(q3 * g, axis=-1, keepdims=True) * v_shift
    for _ in range(1, SUBLANES):
        g = fm * pltpu.roll(g, 1, 1)
        v_shift = pltpu.roll(v_shift, 1, 1)
        o3 = o3 + jnp.sum(q3 * g, axis=-1, keepdims=True) * v_shift
    o_scr[rows, lo:lo + dk] = o3.reshape(n_rows, dk)


def _hgrn_operands(act_scr, rows, h, chunk):
    dk = HGRN_HEAD_DIM
    vdim = HGRN_HEADS * dk
    nb = chunk // SUBLANES
    lo = h * dk
    q = act_scr[rows, lo:lo + dk]
    fg = act_scr[rows, vdim + lo:vdim + lo + dk]
    lf = act_scr[rows, 2 * vdim + lo:2 * vdim + lo + dk]
    v = act_scr[rows, 3 * vdim + lo:3 * vdim + lo + dk]
    k = 1.0 - fg

    shape3 = (nb, SUBLANES, dk)
    q3, k3, v3 = q.reshape(shape3), k.reshape(shape3), v.reshape(shape3)
    ridx = lax.broadcasted_iota(jnp.int32, shape3, 1)

    w3 = lf.reshape(shape3)
    for s in (1, 2, 4):
        w3 = w3 + jnp.where(ridx >= s, pltpu.roll(w3, s, 1), 0.0)
    tot = w3[:, SUBLANES - 1:SUBLANES, :]
    offs = [jnp.zeros((1, 1, dk), F32)]
    for j in range(nb):
        offs.append(offs[-1] + tot[j:j + 1])

    q_blk = q3 * jnp.exp2(w3)
    k_blk = k3 * jnp.exp2(tot - w3)
    if nb > 1:
        q_in = q_blk * jnp.exp2(jnp.concatenate(offs[:nb], axis=0))
        k_end = k_blk * jnp.exp2(jnp.concatenate([offs[nb] - offs[j + 1] for j in range(nb)], axis=0))
    else:
        q_in, k_end = q_blk, k_blk

    k_st = v_st = far_mask = None
    if nb > 1:
        k_rows, v_rows, starts = [], [], []
        n_rows = 0
        for i in range(1, nb):
            starts.append(n_rows)
            for j in range(i):
                k_rows.append(k_blk[j] * jnp.exp2(offs[i][0] - offs[j + 1][0]))
                v_rows.append(v3[j])
                n_rows += SUBLANES
        pad = -n_rows % BF16_ROWS
        if pad:
            k_rows.append(jnp.zeros((pad, dk), F32))
            v_rows.append(jnp.zeros((pad, dk), F32))
        k_st = jnp.concatenate(k_rows, axis=0).astype(BF16)
        v_st = jnp.concatenate(v_rows, axis=0).astype(BF16)
        r_tot = n_rows + pad
        row_blk = lax.broadcasted_iota(jnp.int32, (chunk, r_tot), 0) // SUBLANES
        col = lax.broadcasted_iota(jnp.int32, (chunk, r_tot), 1)
        col_blk = jnp.where(col < n_rows, 1, 0)
        for s0 in starts[1:]:
            col_blk = col_blk + jnp.where(col >= s0, 1, 0)
        far_mask = row_blk == col_blk

    e_tot = jnp.exp2(offs[nb][0])
    e_hi = e_tot.astype(BF16).astype(F32)
    e_mid = (e_tot - e_hi).astype(BF16).astype(F32)
    e_lo = e_tot - e_hi - e_mid
    prow = lax.broadcasted_iota(jnp.int32, (BF16_ROWS, dk), 0)
    pieces = jnp.where(prow == 0, e_hi, jnp.where(prow == 1, e_mid, jnp.where(prow == 2, e_lo, 0.0)))
    lhs_state = jnp.concatenate([k_end.reshape(chunk, dk), pieces], axis=0).astype(BF16)
    v_pad = jnp.concatenate([v, jnp.zeros((BF16_ROWS, dk), F32)], axis=0).astype(BF16)
    return (q_in.reshape(chunk, dk).astype(BF16), q_blk.reshape(chunk, dk).astype(BF16),
            k_st, v_st, far_mask, lhs_state, v_pad)


def _hgrn_chunk(act_scr, yg_scr, state_refs, rows, hn_row, ones_rows_bf, chunk):
    dk = HGRN_HEAD_DIM
    vdim = HGRN_HEADS * dk

    def finish(h, o, p, v_st):
        o = o + yg_scr[rows, h * dk:(h + 1) * dk]
        if p is not None:
            o = o + _dot(p, v_st)
        ms = jnp.mean(o * o, axis=-1, keepdims=True)
        gs = act_scr[rows, 4 * vdim + h * dk:4 * vdim + (h + 1) * dk]
        yg_scr[rows, h * dk:(h + 1) * dk] = ((o * lax.rsqrt(ms + NORM_EPS)) * hn_row[:, h * dk:(h + 1) * dk]) * gs

    all_heads = list(range(HGRN_HEADS))
    if chunk > SUBLANES:
        first = _hgrn_first_matmuls(act_scr, state_refs, rows, ones_rows_bf, chunk, all_heads)
        for h in all_heads:
            _hgrn_near_pairs(act_scr, yg_scr, rows, h)
    else:
        first = {}
        for h in all_heads:
            _hgrn_near_pairs(act_scr, yg_scr, rows, h)
            first.update(_hgrn_first_matmuls(act_scr, state_refs, rows, ones_rows_bf, chunk, [h]))
    for h in all_heads:
        finish(h, *first[h])


def _hgrn_first_matmuls(act_scr, state_refs, rows, ones_rows_bf, chunk, heads):
    dk = HGRN_HEAD_DIM
    ops = {h: _hgrn_operands(act_scr, rows, h, chunk) for h in heads}
    st = {h: state_refs[h][0][...] for h in heads}
    o = {h: _dot(ops[h][0], st[h].astype(BF16)) for h in heads}
    p = {h: None for h in heads}
    if chunk > SUBLANES:
        p = {h: jnp.where(ops[h][4], _dot_nt(ops[h][1], ops[h][2]), 0.0).astype(BF16) for h in heads}
    for h in heads:
        res = _dot_tn(ops[h][5], jnp.concatenate([ops[h][6], ones_rows_bf], axis=1))
        state_refs[h][1][...] = res[:, dk:] * st[h] + res[:, :dk]
    return {h: (o[h], p[h], ops[h][3]) for h in heads}


def _hgrn_normed(x_ref, gmix_ref):
    return _rms(x_ref[...], gmix_ref[...]).astype(BF16)


def _hgrn_project_cols(xn, win_ref, lb_row, act, c0):
    vdim = HGRN_HEADS * HGRN_HEAD_DIM
    blk = _dot(xn, win_ref[:, c0:c0 + PROJ_COLS])
    sect, s0 = divmod(c0, vdim)
    if sect == 0:
        act[:, s0:s0 + PROJ_COLS] = blk * jax.nn.sigmoid(blk)
    elif sect == 1:
        lb = lb_row[:, s0:s0 + PROJ_COLS]
        fgate = lb + (1.0 - lb) * jax.nn.sigmoid(blk)
        act[:, vdim + s0:vdim + s0 + PROJ_COLS] = fgate
        act[:, 2 * vdim + s0:2 * vdim + s0 + PROJ_COLS] = jnp.log2(fgate)
    elif sect == 2:
        act[:, 3 * vdim + s0:3 * vdim + s0 + PROJ_COLS] = blk
    else:
        act[:, 4 * vdim + s0:4 * vdim + s0 + PROJ_COLS] = jax.nn.sigmoid(blk)


def _hgrn_kernel(x_ref, xnext_ref, s0_ref, gmix_ref, win_ref, lbp_ref, hn_ref, wout_ref, y_ref, sout_ref,
                 act_scr, yg_scr, st_scr=None, *, bb, tb, chunk, layer_slot):
    t = pl.program_id(1)
    step = pl.program_id(0) * pl.num_programs(1) + t
    slot = lax.rem(step, 2)
    d = x_ref.shape[-1]
    m = bb * tb
    dk = HGRN_HEAD_DIM
    single_time_block = st_scr is None

    if not single_time_block:
        @pl.when(t == 0)
        def _():
            st_scr[...] = s0_ref[...]

    lbp = lbp_ref[...]
    e = jnp.exp(lbp - jnp.max(lbp, axis=0, keepdims=True))
    sm = e / jnp.sum(e, axis=0, keepdims=True)
    lb_row = sm[0:1]
    for s in range(1, layer_slot + 1):
        lb_row = lb_row + sm[s:s + 1]

    proj_cols = list(range(0, 4 * HGRN_HEADS * dk, PROJ_COLS))

    @pl.when(step == 0)
    def _():
        xn0 = _hgrn_normed(x_ref, gmix_ref)
        for c0 in proj_cols:
            _hgrn_project_cols(xn0, win_ref, lb_row, act_scr.at[0], c0)

    def run(act, act_next):
        xn_next = _hgrn_normed(xnext_ref, gmix_ref)
        hn_row = hn_ref[...]
        orow = lax.broadcasted_iota(jnp.int32, (chunk + BF16_ROWS, dk), 0)
        ones_rows_bf = jnp.where((orow >= chunk) & (orow < chunk + 3), 1.0, 0.0).astype(BF16)
        chunks_per_b = tb // chunk
        n_chunks = m // chunk
        for ci in range(n_chunks):
            for c0 in proj_cols[ci * len(proj_cols) // n_chunks:(ci + 1) * len(proj_cols) // n_chunks]:
                _hgrn_project_cols(xn_next, win_ref, lb_row, act_next, c0)
            bi = ci // chunks_per_b
            if single_time_block:
                state_refs = [(s0_ref.at[bi, h], sout_ref.at[bi, h]) for h in range(HGRN_HEADS)]
            else:
                state_refs = [(st_scr.at[bi, h], st_scr.at[bi, h]) for h in range(HGRN_HEADS)]
            _hgrn_chunk(act, yg_scr, state_refs, slice(ci * chunk, (ci + 1) * chunk), hn_row, ones_rows_bf, chunk)
        y = _dot(yg_scr[...].astype(BF16), wout_ref[...])
        y_ref[...] = x_ref[...] + y

    for parity in (0, 1):
        @pl.when(slot == parity)
        def _():
            run(act_scr.at[parity], act_scr.at[1 - parity])

    if not single_time_block:
        @pl.when(t == pl.num_programs(1) - 1)
        def _():
            sout_ref[...] = st_scr[...]


def _hgrn(x, b, t, s0, gmix, win, lb_params, hnorm, wout, layer, *, layer_slot, bb, tb):
    d = x.shape[1]
    chunk = min(HGRN_MAX_CHUNK, t)
    assert t % tb == 0 and tb % chunk == 0 and b % bb == 0 and chunk % SUBLANES == 0
    heads, dk = HGRN_HEADS, HGRN_HEAD_DIM
    vdim = heads * dk
    assert win.shape[2] == 4 * vdim and vdim % PROJ_COLS == 0
    m = bb * tb
    nj = t // tb
    n_steps = (b // bb) * nj

    assert bb == 1 or tb == t
    this_block = lambda i, j: (i * nj + j, 0)
    next_block = lambda i, j: (jnp.minimum(i * nj + j + 1, n_steps - 1), 0)

    scratch = [pltpu.VMEM((2, m, 5 * vdim), F32), pltpu.VMEM((m, vdim), F32)]
    if nj > 1:
        scratch.append(pltpu.VMEM((bb, heads, dk, dk), F32))
    return pl.pallas_call(
        functools.partial(_hgrn_kernel, bb=bb, tb=tb, chunk=chunk, layer_slot=layer_slot),
        grid=(b // bb, nj),
        in_specs=[
            pl.BlockSpec((m, d), this_block),
            pl.BlockSpec((m, d), next_block),
            pl.BlockSpec((None, bb, heads, dk, dk), lambda i, j: (layer_slot, i, 0, 0, 0)),
            _layer_spec(gmix, layer, 2),
            _layer_spec(win, layer_slot, 2),
            pl.BlockSpec(lb_params.shape, lambda i, j: (0, 0)),
            _layer_spec(hnorm, layer_slot, 2),
            _layer_spec(wout, layer_slot, 2),
        ],
        out_specs=[
            pl.BlockSpec((m, d), this_block),
            pl.BlockSpec((bb, heads, dk, dk), lambda i, j: (i, 0, 0, 0)),
        ],
        out_shape=[
            jax.ShapeDtypeStruct((b * t, d), F32),
            jax.ShapeDtypeStruct((b, heads, dk, dk), F32),
        ],
        scratch_shapes=scratch,
        compiler_params=pltpu.CompilerParams(
            dimension_semantics=("arbitrary", "arbitrary"), vmem_limit_bytes=VMEM_LIMIT_BYTES),
        name="hgrn_mixer",
    )(x, x, s0, gmix, win, lb_params, hnorm, wout)


def _conv_kernel(x_ref, buf_ref, gmix_ref, win_ref, cw_ref, wout_ref, y_ref, bufo_ref, ext_scr, *, bb, tb):
    t = pl.program_id(1)
    d = x_ref.shape[-1]
    m = bb * tb
    halo = SUBLANES
    tail = CONV_W - 1

    @pl.when(t == 0)
    def _():
        ext_scr[:, halo - tail:halo, :] = buf_ref[...]

    x = x_ref[...].reshape(m, d)
    xn = _rms(x, gmix_ref[...]).astype(BF16)
    b_gate = _dot(xn, win_ref[:, 0:d].astype(BF16))
    c_gate = _dot(xn, win_ref[:, d:2 * d].astype(BF16))
    v = _dot(xn, win_ref[:, 2 * d:3 * d].astype(BF16))
    ext_scr[:, halo:halo + tb, :] = (c_gate * v).reshape(bb, tb, d)

    cw = cw_ref[...]
    conv = None
    for tap in range(CONV_W):
        start = halo - tail + tap
        term = cw[tap:tap + 1] * ext_scr[:, start:start + tb, :]
        conv = term if conv is None else conv + term
    y = _dot((b_gate * conv.reshape(m, d)).astype(BF16), wout_ref[...].astype(BF16))
    y_ref[...] = (x + y).reshape(bb, tb, d)

    new_tail = ext_scr[:, halo + tb - tail:halo + tb, :]
    ext_scr[:, halo - tail:halo, :] = new_tail

    @pl.when(t == pl.num_programs(1) - 1)
    def _():
        bufo_ref[...] = new_tail


def _conv(x, buf, gmix, win, cw, wout, layer, *, layer_slot, bb, tb):
    b, t, d = x.shape
    tail = CONV_W - 1
    assert t % tb == 0 and b % bb == 0 and tb >= tail
    return pl.pallas_call(
        functools.partial(_conv_kernel, bb=bb, tb=tb),
        grid=(b // bb, t // tb),
        in_specs=[
            pl.BlockSpec((bb, tb, d), lambda i, j: (i, j, 0)),
            pl.BlockSpec((None, bb, tail, d), lambda i, j: (layer_slot, i, 0, 0)),
            _layer_spec(gmix, layer, 2),
            _layer_spec(win, layer_slot, 2),
            _layer_spec(cw, layer_slot, 2),
            _layer_spec(wout, layer_slot, 2),
        ],
        out_specs=[
            pl.BlockSpec((bb, tb, d), lambda i, j: (i, j, 0)),
            pl.BlockSpec((bb, tail, d), lambda i, j: (i, 0, 0)),
        ],
        out_shape=[
            jax.ShapeDtypeStruct((b, t, d), F32),
            jax.ShapeDtypeStruct((b, tail, d), F32),
        ],
        scratch_shapes=[pltpu.VMEM((bb, tb + SUBLANES, d), F32)],
        compiler_params=pltpu.CompilerParams(
            dimension_semantics=("arbitrary", "arbitrary"), vmem_limit_bytes=VMEM_LIMIT_BYTES),
        name="conv_mixer",
    )(x, buf, gmix, win, cw, wout)


def kernel(x_prompt, x_sample, state_hgrn, state_conv, norm_ffn1, w_ffn1_up, w_ffn1_down, norm_mix, norm_ffn2, w_ffn2_up, w_ffn2_down, w_hgrn_in, hgrn_lower_bounds, hgrn_norm, w_hgrn_out, w_conv_in, conv_w, w_conv_out, norm_final):
    g_ffn1, g_mix, g_ffn2, g_hgrn = _rows(norm_ffn1), _rows(norm_mix), _rows(norm_ffn2), _rows(hgrn_norm)
    w1_up, w1_dn, w2_up, w2_dn = w_ffn1_up, w_ffn1_down, w_ffn2_up, w_ffn2_down
    wc_in, wc_out = w_conv_in, w_conv_out
    wh_in, wh_out = w_hgrn_in.astype(BF16), w_hgrn_out.astype(BF16)
    lb_params = hgrn_lower_bounds.astype(F32)

    (bp, tp, d), (bs, ts, _) = x_prompt.shape, x_sample.shape
    n_hgrn, _, heads, dk, dv = state_hgrn.shape
    n_conv, _, tail, _ = state_conv.shape
    depth = norm_ffn1.shape[0]
    zero_hgrn = jnp.zeros((n_hgrn, bp, heads, dk, dv), x_prompt.dtype)
    zero_conv = jnp.zeros((n_conv, bp, tail, d), x_prompt.dtype)

    xp, xs = x_prompt.reshape(bp * tp, d), x_sample.reshape(bs * ts, d)
    hgrn_p, hgrn_s, conv_p, conv_s = [], [], [], []
    for layer in range(depth):
        xp, xs = _ffn(xp, xs, g_ffn1, w1_up, w1_dn, norm_final, layer, final_norm=False, tm=FFN_ROWS)
        j = layer // 2
        if layer % 2 == 0:
            xp, s = _hgrn(xp, bp, tp, zero_hgrn, g_mix, wh_in, lb_params, g_hgrn, wh_out, layer,
                          layer_slot=j, bb=1, tb=HGRN_PROMPT_ROWS)
            hgrn_p.append(s)
            xs, s = _hgrn(xs, bs, ts, state_hgrn, g_mix, wh_in, lb_params, g_hgrn, wh_out, layer,
                          layer_slot=j, bb=HGRN_SAMPLE_SEQS, tb=ts)
            hgrn_s.append(s)
        else:
            x3, s = _conv(xp.reshape(bp, tp, d), zero_conv, g_mix, wc_in, conv_w, wc_out, layer,
                          layer_slot=j, bb=1, tb=CONV_PROMPT_ROWS)
            xp = x3.reshape(bp * tp, d)
            conv_p.append(s)
            x3, s = _conv(xs.reshape(bs, ts, d), state_conv, g_mix, wc_in, conv_w, wc_out, layer,
                          layer_slot=j, bb=CONV_SAMPLE_SEQS, tb=ts)
            xs = x3.reshape(bs * ts, d)
            conv_s.append(s)
        xp, xs = _ffn(xp, xs, g_ffn2, w2_up, w2_dn, norm_final, layer,
                      final_norm=(layer == depth - 1), tm=FFN_ROWS)
    return (xp.reshape(bp, tp, d), xs.reshape(bs, ts, d), jnp.stack(hgrn_p), jnp.stack(hgrn_s),
            jnp.stack(conv_p), jnp.stack(conv_s))
```

```python
import functools

import jax
import jax.numpy as jnp
from jax import lax
from jax.experimental import pallas as pl
from jax.experimental.pallas import tpu as pltpu

F32 = jnp.float32
BF16 = jnp.bfloat16

NORM_EPS = 1e-6
FFN_RES = 0.5
HGRN_HEADS = 8
HGRN_HEAD_DIM = 128
HGRN_MAX_CHUNK = 32
CONV_W = 3
SUBLANES = 8
BF16_ROWS = 16
LANES = 128
FFN_CHUNK = 256
PROJ_COLS = 512
VMEM_LIMIT_BYTES = 58 * 1024 * 1024
FFN_ROWS = 512
HGRN_PROMPT_ROWS = 512
HGRN_SAMPLE_SEQS = 16
CONV_PROMPT_ROWS = 512
CONV_SAMPLE_SEQS = 16


def _rms(x, gain):
    r = lax.rsqrt(jnp.mean(x * x, axis=-1, keepdims=True) + NORM_EPS)
    return (x * r) * gain


def _dot(a, b):
    return jnp.dot(a, b, preferred_element_type=F32)


def _dot_nt(a, b):
    return lax.dot_general(a, b, (((1,), (1,)), ((), ())), preferred_element_type=F32)


def _dot_tn(a, b):
    return lax.dot_general(a, b, (((0,), (0,)), ((), ())), preferred_element_type=F32)


def _layer_spec(w, layer, n_grid):
    zeros = (0,) * (w.ndim - 1)
    index_map = (lambda i: (layer,) + zeros) if n_grid == 1 else (lambda i, j: (layer,) + zeros)
    return pl.BlockSpec((None,) + w.shape[1:], index_map, pipeline_mode=pl.Buffered(1))


def _rows(p):
    return p.reshape(p.shape[0], 1, p.shape[1])


def _ffn_kernel(xa_ref, xb_ref, g_ref, wup_hbm, wdn_hbm, gf_ref, oa_ref, ob_ref,
                wup_bf, wdn_bf, stage_up, stage_dn, sems, *, layer, n_a, final_norm):
    ff = wdn_bf.shape[0]
    chunk_starts = list(range(0, ff, FFN_CHUNK))

    def chunk_copies(ci):
        c0, slot = chunk_starts[ci], ci % 2
        return (
            pltpu.make_async_copy(wup_hbm.at[layer, :, pl.ds(c0, FFN_CHUNK)], stage_up.at[slot, 0], sems.at[slot, 0]),
            pltpu.make_async_copy(wup_hbm.at[layer, :, pl.ds(ff + c0, FFN_CHUNK)], stage_up.at[slot, 1],
                                  sems.at[slot, 1]),
            pltpu.make_async_copy(wdn_hbm.at[layer, pl.ds(c0, FFN_CHUNK), :], stage_dn.at[slot], sems.at[slot, 2]),
        )

    def tile(x_ref, o_ref, stream_weights):
        x = x_ref[...]
        if stream_weights:
            for copy in chunk_copies(0):
                copy.start()
        xn = _rms(x, g_ref[...]).astype(BF16)
        acc = jnp.zeros_like(x)
        for ci, c0 in enumerate(chunk_starts):
            if stream_weights:
                if ci + 1 < len(chunk_starts):
                    for copy in chunk_copies(ci + 1):
                        copy.start()
                for copy in chunk_copies(ci):
                    copy.wait()
                slot = ci % 2
                wup_bf[:, c0:c0 + FFN_CHUNK] = stage_up[slot, 0].astype(BF16)
                wup_bf[:, ff + c0:ff + c0 + FFN_CHUNK] = stage_up[slot, 1].astype(BF16)
                wdn_bf[c0:c0 + FFN_CHUNK, :] = stage_dn[slot].astype(BF16)
            gate = _dot(xn, wup_bf[:, c0:c0 + FFN_CHUNK])
            up = _dot(xn, wup_bf[:, ff + c0:ff + c0 + FFN_CHUNK])
            act = (gate * jax.nn.sigmoid(gate) * up).astype(BF16)
            acc = acc + _dot(act, wdn_bf[c0:c0 + FFN_CHUNK, :])
        out = x + FFN_RES * acc
        if final_norm:
            out = _rms(out, gf_ref[...])
        o_ref[...] = out

    step = pl.program_id(0)
    pl.when(step == 0)(lambda: tile(xa_ref, oa_ref, True))
    pl.when((step > 0) & (step < n_a))(lambda: tile(xa_ref, oa_ref, False))
    pl.when(step >= n_a)(lambda: tile(xb_ref, ob_ref, False))


def _ffn(xa, xb, gains, wup, wdn, gain_final, layer, *, final_norm, tm):
    (na, d), nb = xa.shape, xb.shape[0]
    ff = wdn.shape[1]
    assert na % tm == 0 and nb % tm == 0 and ff % FFN_CHUNK == 0 and wup.shape[2] == 2 * ff
    n_a, n_b = na // tm, nb // tm
    a_block = lambda i: (jnp.minimum(i, n_a - 1), 0)
    b_block = lambda i: (jnp.maximum(i - n_a, 0), 0)
    return pl.pallas_call(
        functools.partial(_ffn_kernel, layer=layer, n_a=n_a, final_norm=final_norm),
        grid=(n_a + n_b,),
        in_specs=[
            pl.BlockSpec((tm, d), a_block),
            pl.BlockSpec((tm, d), b_block),
            _layer_spec(gains, layer, 1),
            pl.BlockSpec(memory_space=pl.ANY),
            pl.BlockSpec(memory_space=pl.ANY),
            pl.BlockSpec((1, d), lambda i: (0, 0)),
        ],
        out_specs=[pl.BlockSpec((tm, d), a_block), pl.BlockSpec((tm, d), b_block)],
        out_shape=[jax.ShapeDtypeStruct((na, d), F32), jax.ShapeDtypeStruct((nb, d), F32)],
        scratch_shapes=[
            pltpu.VMEM((d, 2 * ff), BF16),
            pltpu.VMEM((ff, d), BF16),
            pltpu.VMEM((2, 2, d, FFN_CHUNK), F32),
            pltpu.VMEM((2, FFN_CHUNK, d), F32),
            pltpu.SemaphoreType.DMA((2, 3)),
        ],
        compiler_params=pltpu.CompilerParams(
            dimension_semantics=("arbitrary",), vmem_limit_bytes=VMEM_LIMIT_BYTES),
        name="ffn_final" if final_norm else "ffn",
    )(xa, xb, gains, wup, wdn, gain_final.reshape(1, d))


def _hgrn_near_pairs(act_scr, o_scr, rows, h):
    dk = HGRN_HEAD_DIM
    vdim = HGRN_HEADS * dk
    lo = h * dk
    q = act_scr[rows, lo:lo + dk]
    fg = act_scr[rows, vdim + lo:vdim + lo + dk]
    v = act_scr[rows, 3 * vdim + lo:3 * vdim + lo + dk]
    n_rows = q.shape[0]
    shape3 = (n_rows // SUBLANES, SUBLANES, dk)
    q3, fg3, v3 = q.reshape(shape3), fg.reshape(shape3), v.reshape(shape3)
    k3 = 1.0 - fg3
    ridx = lax.broadcasted_iota(jnp.int32, shape3, 1)
    fm = jnp.where(ridx == 0, 0.0, fg3)
    g, v_shift = k3, v3
    o3 = jnp.sum(q3 * g, axis=-1, keepdims=True) * v_shift
    for _ in range(1, SUBLANES):
        g = fm * pltpu.roll(g, 1, 1)
        v_shift = pltpu.roll(v_shift, 1, 1)
        o3 = o3 + jnp.sum(q3 * g, axis=-1, keepdims=True) * v_shift
    o_scr[rows, lo:lo + dk] = o3.reshape(n_rows, dk)


def _hgrn_operands(act_scr, rows, h, chunk):
    dk = HGRN_HEAD_DIM
    vdim = HGRN_HEADS * dk
    nb = chunk // SUBLANES
    lo = h * dk
    q = act_scr[rows, lo:lo + dk]
    fg = act_scr[rows, vdim + lo:vdim + lo + dk]
    lf = act_scr[rows, 2 * vdim + lo:2 * vdim + lo + dk]
    v = act_scr[rows, 3 * vdim + lo:3 * vdim + lo + dk]
    k = 1.0 - fg

    shape3 = (nb, SUBLANES, dk)
    q3, k3, v3 = q.reshape(shape3), k.reshape(shape3), v.reshape(shape3)
    ridx = lax.broadcasted_iota(jnp.int32, shape3, 1)

    w3 = lf.reshape(shape3)
    for s in (1, 2, 4):
        w3 = w3 + jnp.where(ridx >= s, pltpu.roll(w3, s, 1), 0.0)
    tot = w3[:, SUBLANES - 1:SUBLANES, :]
    offs = [jnp.zeros((1, 1, dk), F32)]
    for j in range(nb):
        offs.append(offs[-1] + tot[j:j + 1])

    q_blk = q3 * jnp.exp2(w3)
    k_blk = k3 * jnp.exp2(tot - w3)
    if nb > 1:
        q_in = q_blk * jnp.exp2(jnp.concatenate(offs[:nb], axis=0))
        k_end = k_blk * jnp.exp2(jnp.concatenate([offs[nb] - offs[j + 1] for j in range(nb)], axis=0))
    else:
        q_in, k_end = q_blk, k_blk

    k_st = v_st = far_mask = None
    if nb > 1:
        k_rows, v_rows, starts = [], [], []
        n_rows = 0
        for i in range(1, nb):
            starts.append(n_rows)
            for j in range(i):
                k_rows.append(k_blk[j] * jnp.exp2(offs[i][0] - offs[j + 1][0]))
                v_rows.append(v3[j])
                n_rows += SUBLANES
        pad = -n_rows % BF16_ROWS
        if pad:
            k_rows.append(jnp.zeros((pad, dk), F32))
            v_rows.append(jnp.zeros((pad, dk), F32))
        k_st = jnp.concatenate(k_rows, axis=0).astype(BF16)
        v_st = jnp.concatenate(v_rows, axis=0).astype(BF16)
        r_tot = n_rows + pad
        row_blk = lax.broadcasted_iota(jnp.int32, (chunk, r_tot), 0) // SUBLANES
        col = lax.broadcasted_iota(jnp.int32, (chunk, r_tot), 1)
        col_blk = jnp.where(col < n_rows, 1, 0)
        for s0 in starts[1:]:
            col_blk = col_blk + jnp.where(col >= s0, 1, 0)
        far_mask = row_blk == col_blk

    e_tot = jnp.exp2(offs[nb][0])
    e_hi = e_tot.astype(BF16).astype(F32)
    e_mid = (e_tot - e_hi).astype(BF16).astype(F32)
    e_lo = e_tot - e_hi - e_mid
    prow = lax.broadcasted_iota(jnp.int32, (BF16_ROWS, dk), 0)
    pieces = jnp.where(prow == 0, e_hi, jnp.where(prow == 1, e_mid, jnp.where(prow == 2, e_lo, 0.0)))
    lhs_state = jnp.concatenate([k_end.reshape(chunk, dk), pieces], axis=0).astype(BF16)
    v_pad = jnp.concatenate([v, jnp.zeros((BF16_ROWS, dk), F32)], axis=0).astype(BF16)
    return (q_in.reshape(chunk, dk).astype(BF16), q_blk.reshape(chunk, dk).astype(BF16),
            k_st, v_st, far_mask, lhs_state, v_pad)


def _hgrn_chunk(act_scr, yg_scr, state_refs, rows, hn_row, ones_rows_bf, chunk):
    dk = HGRN_HEAD_DIM
    vdim = HGRN_HEADS * dk

    def finish(h, o, p, v_st):
        o = o + yg_scr[rows, h * dk:(h + 1) * dk]
        if p is not None:
            o = o + _dot(p, v_st)
        ms = jnp.mean(o * o, axis=-1, keepdims=True)
        gs = act_scr[rows, 4 * vdim + h * dk:4 * vdim + (h + 1) * dk]
        yg_scr[rows, h * dk:(h + 1) * dk] = ((o * lax.rsqrt(ms + NORM_EPS)) * hn_row[:, h * dk:(h + 1) * dk]) * gs

    all_heads = list(range(HGRN_HEADS))
    if chunk > SUBLANES:
        first = _hgrn_first_matmuls(act_scr, state_refs, rows, ones_rows_bf, chunk, all_heads)
        for h in all_heads:
            _hgrn_near_pairs(act_scr, yg_scr, rows, h)
    else:
        first = {}
        for h in all_heads:
            _hgrn_near_pairs(act_scr, yg_scr, rows, h)
            first.update(_hgrn_first_matmuls(act_scr, state_refs, rows, ones_rows_bf, chunk, [h]))
    for h in all_heads:
        finish(h, *first[h])


def _hgrn_first_matmuls(act_scr, state_refs, rows, ones_rows_bf, chunk, heads):
    dk = HGRN_HEAD_DIM
    ops = {h: _hgrn_operands(act_scr, rows, h, chunk) for h in heads}
    st = {h: state_refs[h][0][...] for h in heads}
    o = {h: _dot(ops[h][0], st[h].astype(BF16)) for h in heads}
    p = {h: None for h in heads}
    if chunk > SUBLANES:
        p = {h: jnp.where(ops[h][4], _dot_nt(ops[h][1], ops[h][2]), 0.0).astype(BF16) for h in heads}
    for h in heads:
        res = _dot_tn(ops[h][5], jnp.concatenate([ops[h][6], ones_rows_bf], axis=1))
        state_refs[h][1][...] = res[:, dk:] * st[h] + res[:, :dk]
    return {h: (o[h], p[h], ops[h][3]) for h in heads}


def _hgrn_normed(x_ref, gmix_ref):
    return _rms(x_ref[...], gmix_ref[...]).astype(BF16)


def _hgrn_project_cols(xn, win_ref, lb_row, act, c0):
    vdim = HGRN_HEADS * HGRN_HEAD_DIM
    blk = _dot(xn, win_ref[:, c0:c0 + PROJ_COLS])
    sect, s0 = divmod(c0, vdim)
    if sect == 0:
        act[:, s0:s0 + PROJ_COLS] = blk * jax.nn.sigmoid(blk)
    elif sect == 1:
        lb = lb_row[:, s0:s0 + PROJ_COLS]
        fgate = lb + (1.0 - lb) * jax.nn.sigmoid(blk)
        act[:, vdim + s0:vdim + s0 + PROJ_COLS] = fgate
        act[:, 2 * vdim + s0:2 * vdim + s0 + PROJ_COLS] = jnp.log2(fgate)
    elif sect == 2:
        act[:, 3 * vdim + s0:3 * vdim + s0 + PROJ_COLS] = blk
    else:
        act[:, 4 * vdim + s0:4 * vdim + s0 + PROJ_COLS] = jax.nn.sigmoid(blk)


def _hgrn_kernel(x_ref, xnext_ref, s0_ref, gmix_ref, win_ref, lbp_ref, hn_ref, wout_ref, y_ref, sout_ref,
                 act_scr, yg_scr, st_scr=None, *, bb, tb, chunk, layer_slot):
    t = pl.program_id(1)
    step = pl.program_id(0) * pl.num_programs(1) + t
    slot = lax.rem(step, 2)
    d = x_ref.shape[-1]
    m = bb * tb
    dk = HGRN_HEAD_DIM
    single_time_block = st_scr is None

    if not single_time_block:
        @pl.when(t == 0)
        def _():
            st_scr[...] = s0_ref[...]

    lbp = lbp_ref[...]
    e = jnp.exp(lbp - jnp.max(lbp, axis=0, keepdims=True))
    sm = e / jnp.sum(e, axis=0, keepdims=True)
    lb_row = sm[0:1]
    for s in range(1, layer_slot + 1):
        lb_row = lb_row + sm[s:s + 1]

    proj_cols = list(range(0, 4 * HGRN_HEADS * dk, PROJ_COLS))

    @pl.when(step == 0)
    def _():
        xn0 = _hgrn_normed(x_ref, gmix_ref)
        for c0 in proj_cols:
            _hgrn_project_cols(xn0, win_ref, lb_row, act_scr.at[0], c0)

    def run(act, act_next):
        xn_next = _hgrn_normed(xnext_ref, gmix_ref)
        hn_row = hn_ref[...]
        orow = lax.broadcasted_iota(jnp.int32, (chunk + BF16_ROWS, dk), 0)
        ones_rows_bf = jnp.where((orow >= chunk) & (orow < chunk + 3), 1.0, 0.0).astype(BF16)
        chunks_per_b = tb // chunk
        n_chunks = m // chunk
        for ci in range(n_chunks):
            for c0 in proj_cols[ci * len(proj_cols) // n_chunks:(ci + 1) * len(proj_cols) // n_chunks]:
                _hgrn_project_cols(xn_next, win_ref, lb_row, act_next, c0)
            bi = ci // chunks_per_b
            if single_time_block:
                state_refs = [(s0_ref.at[bi, h], sout_ref.at[bi, h]) for h in range(HGRN_HEADS)]
            else:
                state_refs = [(st_scr.at[bi, h], st_scr.at[bi, h]) for h in range(HGRN_HEADS)]
            _hgrn_chunk(act, yg_scr, state_refs, slice(ci * chunk, (ci + 1) * chunk), hn_row, ones_rows_bf, chunk)
        y = _dot(yg_scr[...].astype(BF16), wout_ref[...])
        y_ref[...] = x_ref[...] + y

    for parity in (0, 1):
        @pl.when(slot == parity)
        def _():
            run(act_scr.at[parity], act_scr.at[1 - parity])

    if not single_time_block:
        @pl.when(t == pl.num_programs(1) - 1)
        def _():
            sout_ref[...] = st_scr[...]


def _hgrn(x, b, t, s0, gmix, win, lb_params, hnorm, wout, layer, *, layer_slot, bb, tb):
    d = x.shape[1]
    chunk = min(HGRN_MAX_CHUNK, t)
    assert t % tb == 0 and tb % chunk == 0 and b % bb == 0 and chunk % SUBLANES == 0
    heads, dk = HGRN_HEADS, HGRN_HEAD_DIM
    vdim = heads * dk
    assert win.shape[2] == 4 * vdim and vdim % PROJ_COLS == 0
    m = bb * tb
    nj = t // tb
    n_steps = (b // bb) * nj

    assert bb == 1 or tb == t
    this_block = lambda i, j: (i * nj + j, 0)
    next_block = lambda i, j: (jnp.minimum(i * nj + j + 1, n_steps - 1), 0)

    scratch = [pltpu.VMEM((2, m, 5 * vdim), F32), pltpu.VMEM((m, vdim), F32)]
    if nj > 1:
        scratch.append(pltpu.VMEM((bb, heads, dk, dk), F32))
    return pl.pallas_call(
        functools.partial(_hgrn_kernel, bb=bb, tb=tb, chunk=chunk, layer_slot=layer_slot),
        grid=(b // bb, nj),
        in_specs=[
            pl.BlockSpec((m, d), this_block),
            pl.BlockSpec((m, d), next_block),
            pl.BlockSpec((None, bb, heads, dk, dk), lambda i, j: (layer_slot, i, 0, 0, 0)),
            _layer_spec(gmix, layer, 2),
            _layer_spec(win, layer_slot, 2),
            pl.BlockSpec(lb_params.shape, lambda i, j: (0, 0)),
            _layer_spec(hnorm, layer_slot, 2),
            _layer_spec(wout, layer_slot, 2),
        ],
        out_specs=[
            pl.BlockSpec((m, d), this_block),
            pl.BlockSpec((bb, heads, dk, dk), lambda i, j: (i, 0, 0, 0)),
        ],
        out_shape=[
            jax.ShapeDtypeStruct((b * t, d), F32),
            jax.ShapeDtypeStruct((b, heads, dk, dk), F32),
        ],
        scratch_shapes=scratch,
        compiler_params=pltpu.CompilerParams(
            dimension_semantics=("arbitrary", "arbitrary"), vmem_limit_bytes=VMEM_LIMIT_BYTES),
        name="hgrn_mixer",
    )(x, x, s0, gmix, win, lb_params, hnorm, wout)


def _conv_kernel(x_ref, buf_ref, gmix_ref, win_ref, cw_ref, wout_ref, y_ref, bufo_ref, ext_scr, *, bb, tb):
    t = pl.program_id(1)
    d = x_ref.shape[-1]
    m = bb * tb
    halo = SUBLANES
    tail = CONV_W - 1

    @pl.when(t == 0)
    def _():
        ext_scr[:, halo - tail:halo, :] = buf_ref[...]

    x = x_ref[...].reshape(m, d)
    xn = _rms(x, gmix_ref[...]).astype(BF16)
    b_gate = _dot(xn, win_ref[:, 0:d].astype(BF16))
    c_gate = _dot(xn, win_ref[:, d:2 * d].astype(BF16))
    v = _dot(xn, win_ref[:, 2 * d:3 * d].astype(BF16))
    ext_scr[:, halo:halo + tb, :] = (c_gate * v).reshape(bb, tb, d)

    cw = cw_ref[...]
    conv = None
    for tap in range(CONV_W):
        start = halo - tail + tap
        term = cw[tap:tap + 1] * ext_scr[:, start:start + tb, :]
        conv = term if conv is None else conv + term
    y = _dot((b_gate * conv.reshape(m, d)).astype(BF16), wout_ref[...].astype(BF16))
    y_ref[...] = (x + y).reshape(bb, tb, d)

    new_tail = ext_scr[:, halo + tb - tail:halo + tb, :]
    ext_scr[:, halo - tail:halo, :] = new_tail

    @pl.when(t == pl.num_programs(1) - 1)
    def _():
        bufo_ref[...] = new_tail


def _conv(x, buf, gmix, win, cw, wout, layer, *, layer_slot, bb, tb):
    b, t, d = x.shape
    tail = CONV_W - 1
    assert t % tb == 0 and b % bb == 0 and tb >= tail
    return pl.pallas_call(
        functools.partial(_conv_kernel, bb=bb, tb=tb),
        grid=(b // bb, t // tb),
        in_specs=[
            pl.BlockSpec((bb, tb, d), lambda i, j: (i, j, 0)),
            pl.BlockSpec((None, bb, tail, d), lambda i, j: (layer_slot, i, 0, 0)),
            _layer_spec(gmix, layer, 2),
            _layer_spec(win, layer_slot, 2),
            _layer_spec(cw, layer_slot, 2),
            _layer_spec(wout, layer_slot, 2),
        ],
        out_specs=[
            pl.BlockSpec((bb, tb, d), lambda i, j: (i, j, 0)),
            pl.BlockSpec((bb, tail, d), lambda i, j: (i, 0, 0)),
        ],
        out_shape=[
            jax.ShapeDtypeStruct((b, t, d), F32),
            jax.ShapeDtypeStruct((b, tail, d), F32),
        ],
        scratch_shapes=[pltpu.VMEM((bb, tb + SUBLANES, d), F32)],
        compiler_params=pltpu.CompilerParams(
            dimension_semantics=("arbitrary", "arbitrary"), vmem_limit_bytes=VMEM_LIMIT_BYTES),
        name="conv_mixer",
    )(x, buf, gmix, win, cw, wout)


def kernel(x_prompt, x_sample, state_hgrn, state_conv, norm_ffn1, w_ffn1_up, w_ffn1_down, norm_mix, norm_ffn2, w_ffn2_up, w_ffn2_down, w_hgrn_in, hgrn_lower_bounds, hgrn_norm, w_hgrn_out, w_conv_in, conv_w, w_conv_out, norm_final):
    g_ffn1, g_mix, g_ffn2, g_hgrn = _rows(norm_ffn1), _rows(norm_mix), _rows(norm_ffn2), _rows(hgrn_norm)
    w1_up, w1_dn, w2_up, w2_dn = w_ffn1_up, w_ffn1_down, w_ffn2_up, w_ffn2_down
    wc_in, wc_out = w_conv_in, w_conv_out
    wh_in, wh_out = w_hgrn_in.astype(BF16), w_hgrn_out.astype(BF16)
    lb_params = hgrn_lower_bounds.astype(F32)

    (bp, tp, d), (bs, ts, _) = x_prompt.shape, x_sample.shape
    n_hgrn, _, heads, dk, dv = state_hgrn.shape
    n_conv, _, tail, _ = state_conv.shape
    depth = norm_ffn1.shape[0]
    zero_hgrn = jnp.zeros((n_hgrn, bp, heads, dk, dv), x_prompt.dtype)
    zero_conv = jnp.zeros((n_conv, bp, tail, d), x_prompt.dtype)

    xp, xs = x_prompt.reshape(bp * tp, d), x_sample.reshape(bs * ts, d)
    hgrn_p, hgrn_s, conv_p, conv_s = [], [], [], []
    for layer in range(depth):
        xp, xs = _ffn(xp, xs, g_ffn1, w1_up, w1_dn, norm_final, layer, final_norm=False, tm=FFN_ROWS)
        j = layer // 2
        if layer % 2 == 0:
            xp, s = _hgrn(xp, bp, tp, zero_hgrn, g_mix, wh_in, lb_params, g_hgrn, wh_out, layer,
                          layer_slot=j, bb=1, tb=HGRN_PROMPT_ROWS)
            hgrn_p.append(s)
            xs, s = _hgrn(xs, bs, ts, state_hgrn, g_mix, wh_in, lb_params, g_hgrn, wh_out, layer,
                          layer_slot=j, bb=HGRN_SAMPLE_SEQS, tb=ts)
            hgrn_s.append(s)
        else:
            x3, s = _conv(xp.reshape(bp, tp, d), zero_conv, g_mix, wc_in, conv_w, wc_out, layer,
                          layer_slot=j, bb=1, tb=CONV_PROMPT_ROWS)
            xp = x3.reshape(bp * tp, d)
            conv_p.append(s)
            x3, s = _conv(xs.reshape(bs, ts, d), state_conv, g_mix, wc_in, conv_w, wc_out, layer,
                          layer_slot=j, bb=CONV_SAMPLE_SEQS, tb=ts)
            xs = x3.reshape(bs * ts, d)
            conv_s.append(s)
        xp, xs = _ffn(xp, xs, g_ffn2, w2_up, w2_dn, norm_final, layer,
                      final_norm=(layer == depth - 1), tm=FFN_ROWS)
    return (xp.reshape(bp, tp, d), xs.reshape(bs, ts, d), jnp.stack(hgrn_p), jnp.stack(hgrn_s),
            jnp.stack(conv_p), jnp.stack(conv_s))
```

```python
import functools

import jax
import jax.numpy as jnp
from jax import lax
from jax.experimental import pallas as pl
from jax.experimental.pallas import tpu as pltpu

F32 = jnp.float32
BF16 = jnp.bfloat16

NORM_EPS = 1e-6
FFN_RES = 0.5
HGRN_HEADS = 8
HGRN_HEAD_DIM = 128
HGRN_MAX_CHUNK = 32
CONV_W = 3
SUBLANES = 8
BF16_ROWS = 16
LANES = 128
FFN_CHUNK = 256
PROJ_COLS = 512
VMEM_LIMIT_BYTES = 60 * 1024 * 1024
FFN_ROWS = 512
HGRN_PROMPT_ROWS = 512
HGRN_SAMPLE_SEQS = 16
CONV_PROMPT_ROWS = 512
CONV_SAMPLE_SEQS = 16


def _rms(x, gain):
    r = lax.rsqrt(jnp.mean(x * x, axis=-1, keepdims=True) + NORM_EPS)
    return (x * r) * gain


def _dot(a, b):
    return jnp.dot(a, b, preferred_element_type=F32)


def _dot_nt(a, b):
    return lax.dot_general(a, b, (((1,), (1,)), ((), ())), preferred_element_type=F32)


def _dot_tn(a, b):
    return lax.dot_general(a, b, (((0,), (0,)), ((), ())), preferred_element_type=F32)


def _layer_spec(w, layer, n_grid):
    zeros = (0,) * (w.ndim - 1)
    index_map = (lambda i: (layer,) + zeros) if n_grid == 1 else (lambda i, j: (layer,) + zeros)
    return pl.BlockSpec((None,) + w.shape[1:], index_map, pipeline_mode=pl.Buffered(1))


def _rows(p):
    return p.reshape(p.shape[0], 1, p.shape[1])


def _ffn_kernel(xa_ref, xb_ref, g_ref, wup_hbm, wdn_hbm, gf_ref, oa_ref, ob_ref,
                wup_bf, wdn_bf, stage_up, stage_dn, sems, *, layer, n_a, final_norm):
    ff = wdn_bf.shape[0]
    chunk_starts = list(range(0, ff, FFN_CHUNK))

    def chunk_copies(ci):
        c0, slot = chunk_starts[ci], ci % 2
        return (
            pltpu.make_async_copy(wup_hbm.at[layer, :, pl.ds(c0, FFN_CHUNK)], stage_up.at[slot, 0], sems.at[slot, 0]),
            pltpu.make_async_copy(wup_hbm.at[layer, :, pl.ds(ff + c0, FFN_CHUNK)], stage_up.at[slot, 1],
                                  sems.at[slot, 1]),
            pltpu.make_async_copy(wdn_hbm.at[layer, pl.ds(c0, FFN_CHUNK), :], stage_dn.at[slot], sems.at[slot, 2]),
        )

    def tile(x_ref, o_ref, stream_weights):
        x = x_ref[...]
        if stream_weights:
            for copy in chunk_copies(0):
                copy.start()
        xn = _rms(x, g_ref[...]).astype(BF16)
        acc = jnp.zeros_like(x)
        for ci, c0 in enumerate(chunk_starts):
            if stream_weights:
                if ci + 1 < len(chunk_starts):
                    for copy in chunk_copies(ci + 1):
                        copy.start()
                for copy in chunk_copies(ci):
                    copy.wait()
                slot = ci % 2
                wup_bf[:, c0:c0 + FFN_CHUNK] = stage_up[slot, 0].astype(BF16)
                wup_bf[:, ff + c0:ff + c0 + FFN_CHUNK] = stage_up[slot, 1].astype(BF16)
                wdn_bf[c0:c0 + FFN_CHUNK, :] = stage_dn[slot].astype(BF16)
            gate = _dot(xn, wup_bf[:, c0:c0 + FFN_CHUNK])
            up = _dot(xn, wup_bf[:, ff + c0:ff + c0 + FFN_CHUNK])
            act = (gate * jax.nn.sigmoid(gate) * up).astype(BF16)
            acc = acc + _dot(act, wdn_bf[c0:c0 + FFN_CHUNK, :])
        out = x + FFN_RES * acc
        if final_norm:
            out = _rms(out, gf_ref[...])
        o_ref[...] = out

    step = pl.program_id(0)
    pl.when(step == 0)(lambda: tile(xa_ref, oa_ref, True))
    pl.when((step > 0) & (step < n_a))(lambda: tile(xa_ref, oa_ref, False))
    pl.when(step >= n_a)(lambda: tile(xb_ref, ob_ref, False))


def _ffn(xa, xb, gains, wup, wdn, gain_final, layer, *, final_norm, tm):
    (na, d), nb = xa.shape, xb.shape[0]
    ff = wdn.shape[1]
    assert na % tm == 0 and nb % tm == 0 and ff % FFN_CHUNK == 0 and wup.shape[2] == 2 * ff
    n_a, n_b = na // tm, nb // tm
    a_block = lambda i: (jnp.minimum(i, n_a - 1), 0)
    b_block = lambda i: (jnp.maximum(i - n_a, 0), 0)
    return pl.pallas_call(
        functools.partial(_ffn_kernel, layer=layer, n_a=n_a, final_norm=final_norm),
        grid=(n_a + n_b,),
        in_specs=[
            pl.BlockSpec((tm, d), a_block),
            pl.BlockSpec((tm, d), b_block),
            _layer_spec(gains, layer, 1),
            pl.BlockSpec(memory_space=pl.ANY),
            pl.BlockSpec(memory_space=pl.ANY),
            pl.BlockSpec((1, d), lambda i: (0, 0)),
        ],
        out_specs=[pl.BlockSpec((tm, d), a_block), pl.BlockSpec((tm, d), b_block)],
        out_shape=[jax.ShapeDtypeStruct((na, d), F32), jax.ShapeDtypeStruct((nb, d), F32)],
        scratch_shapes=[
            pltpu.VMEM((d, 2 * ff), BF16),
            pltpu.VMEM((ff, d), BF16),
            pltpu.VMEM((2, 2, d, FFN_CHUNK), F32),
            pltpu.VMEM((2, FFN_CHUNK, d), F32),
            pltpu.SemaphoreType.DMA((2, 3)),
        ],
        compiler_params=pltpu.CompilerParams(
            dimension_semantics=("arbitrary",), vmem_limit_bytes=VMEM_LIMIT_BYTES),
        name="ffn_final" if final_norm else "ffn",
    )(xa, xb, gains, wup, wdn, gain_final.reshape(1, d))


def _hgrn_near_pairs(act_scr, o_scr, rows, h):
    dk = HGRN_HEAD_DIM
    vdim = HGRN_HEADS * dk
    lo = h * dk
    q = act_scr[rows, lo:lo + dk]
    fg = act_scr[rows, vdim + lo:vdim + lo + dk]
    v = act_scr[rows, 3 * vdim + lo:3 * vdim + lo + dk]
    n_rows = q.shape[0]
    shape3 = (n_rows // SUBLANES, SUBLANES, dk)
    q3, fg3, v3 = q.reshape(shape3), fg.reshape(shape3), v.reshape(shape3)
    k3 = 1.0 - fg3
    ridx = lax.broadcasted_iota(jnp.int32, shape3, 1)
    fm = jnp.where(ridx == 0, 0.0, fg3)
    g, v_shift = k3, v3
    o3 = jnp.sum(q3 * g, axis=-1, keepdims=True) * v_shift
    for _ in range(1, SUBLANES):
        g = fm * pltpu.roll(g, 1, 1)
        v_shift = pltpu.roll(v_shift, 1, 1)
        o3 = o3 + jnp.sum(q3 * g, axis=-1, keepdims=True) * v_shift
    o_scr[rows, lo:lo + dk] = o3.reshape(n_rows, dk)


def _hgrn_operands(act_scr, rows, h, chunk):
    dk = HGRN_HEAD_DIM
    vdim = HGRN_HEADS * dk
    nb = chunk // SUBLANES
    lo = h * dk
    q = act_scr[rows, lo:lo + dk]
    fg = act_scr[rows, vdim + lo:vdim + lo + dk]
    lf = act_scr[rows, 2 * vdim + lo:2 * vdim + lo + dk]
    v = act_scr[rows, 3 * vdim + lo:3 * vdim + lo + dk]
    k = 1.0 - fg

    shape3 = (nb, SUBLANES, dk)
    q3, k3, v3 = q.reshape(shape3), k.reshape(shape3), v.reshape(shape3)
    ridx = lax.broadcasted_iota(jnp.int32, shape3, 1)

    w3 = lf.reshape(shape3)
    for s in (1, 2, 4):
        w3 = w3 + jnp.where(ridx >= s, pltpu.roll(w3, s, 1), 0.0)
    tot = w3[:, SUBLANES - 1:SUBLANES, :]
    offs = [jnp.zeros((1, 1, dk), F32)]
    for j in range(nb):
        offs.append(offs[-1] + tot[j:j + 1])

    q_blk = q3 * jnp.exp2(w3)
    k_blk = k3 * jnp.exp2(tot - w3)
    if nb > 1:
        q_in = q_blk * jnp.exp2(jnp.concatenate(offs[:nb], axis=0))
        k_end = k_blk * jnp.exp2(jnp.concatenate([offs[nb] - offs[j + 1] for j in range(nb)], axis=0))
    else:
        q_in, k_end = q_blk, k_blk

    k_st = v_st = far_mask = None
    if nb > 1:
        k_rows, v_rows, starts = [], [], []
        n_rows = 0
        for i in range(1, nb):
            starts.append(n_rows)
            for j in range(i):
                k_rows.append(k_blk[j] * jnp.exp2(offs[i][0] - offs[j + 1][0]))
                v_rows.append(v3[j])
                n_rows += SUBLANES
        pad = -n_rows % BF16_ROWS
        if pad:
            k_rows.append(jnp.zeros((pad, dk), F32))
            v_rows.append(jnp.zeros((pad, dk), F32))
        k_st = jnp.concatenate(k_rows, axis=0).astype(BF16)
        v_st = jnp.concatenate(v_rows, axis=0).astype(BF16)
        r_tot = n_rows + pad
        row_blk = lax.broadcasted_iota(jnp.int32, (chunk, r_tot), 0) // SUBLANES
        col = lax.broadcasted_iota(jnp.int32, (chunk, r_tot), 1)
        col_blk = jnp.where(col < n_rows, 1, 0)
        for s0 in starts[1:]:
            col_blk = col_blk + jnp.where(col >= s0, 1, 0)
        far_mask = row_blk == col_blk

    e_tot = jnp.exp2(offs[nb][0])
    e_hi = e_tot.astype(BF16).astype(F32)
    e_mid = (e_tot - e_hi).astype(BF16).astype(F32)
    e_lo = e_tot - e_hi - e_mid
    prow = lax.broadcasted_iota(jnp.int32, (BF16_ROWS, dk), 0)
    pieces = jnp.where(prow == 0, e_hi, jnp.where(prow == 1, e_mid, jnp.where(prow == 2, e_lo, 0.0)))
    lhs_state = jnp.concatenate([k_end.reshape(chunk, dk), pieces], axis=0).astype(BF16)
    v_pad = jnp.concatenate([v, jnp.zeros((BF16_ROWS, dk), F32)], axis=0).astype(BF16)
    return (q_in.reshape(chunk, dk).astype(BF16), q_blk.reshape(chunk, dk).astype(BF16),
            k_st, v_st, far_mask, lhs_state, v_pad)


def _hgrn_chunk(act_scr, yg_scr, state_refs, rows, hn_row, ones_rows_bf, chunk):
    dk = HGRN_HEAD_DIM
    vdim = HGRN_HEADS * dk

    def finish(h, o, p, v_st):
        o = o + yg_scr[rows, h * dk:(h + 1) * dk]
        if p is not None:
            o = o + _dot(p, v_st)
        ms = jnp.mean(o * o, axis=-1, keepdims=True)
        gs = act_scr[rows, 4 * vdim + h * dk:4 * vdim + (h + 1) * dk]
        yg_scr[rows, h * dk:(h + 1) * dk] = ((o * lax.rsqrt(ms + NORM_EPS)) * hn_row[:, h * dk:(h + 1) * dk]) * gs

    all_heads = list(range(HGRN_HEADS))
    if chunk > SUBLANES:
        first = _hgrn_first_matmuls(act_scr, state_refs, rows, ones_rows_bf, chunk, all_heads)
        for h in all_heads:
            _hgrn_near_pairs(act_scr, yg_scr, rows, h)
    else:
        first = {}
        for h in all_heads:
            _hgrn_near_pairs(act_scr, yg_scr, rows, h)
            first.update(_hgrn_first_matmuls(act_scr, state_refs, rows, ones_rows_bf, chunk, [h]))
    for h in all_heads:
        finish(h, *first[h])


def _hgrn_first_matmuls(act_scr, state_refs, rows, ones_rows_bf, chunk, heads):
    dk = HGRN_HEAD_DIM
    ops = {h: _hgrn_operands(act_scr, rows, h, chunk) for h in heads}
    st = {h: state_refs[h][0][...] for h in heads}
    o = {h: _dot(ops[h][0], st[h].astype(BF16)) for h in heads}
    p = {h: None for h in heads}
    if chunk > SUBLANES:
        p = {h: jnp.where(ops[h][4], _dot_nt(ops[h][1], ops[h][2]), 0.0).astype(BF16) for h in heads}
    for h in heads:
        res = _dot_tn(ops[h][5], jnp.concatenate([ops[h][6], ones_rows_bf], axis=1))
        state_refs[h][1][...] = res[:, dk:] * st[h] + res[:, :dk]
    return {h: (o[h], p[h], ops[h][3]) for h in heads}


def _hgrn_normed(x_ref, gmix_ref):
    return _rms(x_ref[...], gmix_ref[...]).astype(BF16)


def _hgrn_project_cols(xn, win_ref, lb_row, act, c0):
    vdim = HGRN_HEADS * HGRN_HEAD_DIM
    blk = _dot(xn, win_ref[:, c0:c0 + PROJ_COLS])
    sect, s0 = divmod(c0, vdim)
    if sect == 0:
        act[:, s0:s0 + PROJ_COLS] = blk * jax.nn.sigmoid(blk)
    elif sect == 1:
        lb = lb_row[:, s0:s0 + PROJ_COLS]
        fgate = lb + (1.0 - lb) * jax.nn.sigmoid(blk)
        act[:, vdim + s0:vdim + s0 + PROJ_COLS] = fgate
        act[:, 2 * vdim + s0:2 * vdim + s0 + PROJ_COLS] = jnp.log2(fgate)
    elif sect == 2:
        act[:, 3 * vdim + s0:3 * vdim + s0 + PROJ_COLS] = blk
    else:
        act[:, 4 * vdim + s0:4 * vdim + s0 + PROJ_COLS] = jax.nn.sigmoid(blk)


def _hgrn_kernel(x_ref, xnext_ref, s0_ref, gmix_ref, win_hbm, lbp_ref, hn_ref, wout_hbm, y_ref, sout_ref,
                 act_scr, yg_scr, win_ref, wout_ref, stage, sems, st_scr=None, *, bb, tb, chunk, layer_slot):
    t = pl.program_id(1)
    step = pl.program_id(0) * pl.num_programs(1) + t
    slot = lax.rem(step, 2)
    d = x_ref.shape[-1]
    m = bb * tb
    dk = HGRN_HEAD_DIM
    single_time_block = st_scr is None

    if not single_time_block:
        @pl.when(t == 0)
        def _():
            st_scr[...] = s0_ref[...]

    lbp = lbp_ref[...]
    e = jnp.exp(lbp - jnp.max(lbp, axis=0, keepdims=True))
    sm = e / jnp.sum(e, axis=0, keepdims=True)
    lb_row = sm[0:1]
    for s in range(1, layer_slot + 1):
        lb_row = lb_row + sm[s:s + 1]

    proj_cols = list(range(0, 4 * HGRN_HEADS * dk, PROJ_COLS))

    slabs = [(win_hbm, win_ref, c0) for c0 in proj_cols] + \
            [(wout_hbm, wout_ref, c0) for c0 in range(0, wout_ref.shape[1], PROJ_COLS)]

    def slab_copy(k):
        src, _, c0 = slabs[k]
        return pltpu.make_async_copy(src.at[layer_slot, :, pl.ds(c0, PROJ_COLS)], stage.at[k % 2], sems.at[k % 2])

    @pl.when(step == 0)
    def _():
        slab_copy(0).start()
        xn0 = _hgrn_normed(x_ref, gmix_ref)
        for k, (_, dst, c0) in enumerate(slabs):
            if k + 1 < len(slabs):
                slab_copy(k + 1).start()
            slab_copy(k).wait()
            dst[:, c0:c0 + PROJ_COLS] = stage[k % 2].astype(BF16)
            if dst is win_ref:
                _hgrn_project_cols(xn0, win_ref, lb_row, act_scr.at[0], c0)

    def run(act, act_next):
        xn_next = _hgrn_normed(xnext_ref, gmix_ref)
        hn_row = hn_ref[...]
        orow = lax.broadcasted_iota(jnp.int32, (chunk + BF16_ROWS, dk), 0)
        ones_rows_bf = jnp.where((orow >= chunk) & (orow < chunk + 3), 1.0, 0.0).astype(BF16)
        chunks_per_b = tb // chunk
        n_chunks = m // chunk
        for ci in range(n_chunks):
            for c0 in proj_cols[ci * len(proj_cols) // n_chunks:(ci + 1) * len(proj_cols) // n_chunks]:
                _hgrn_project_cols(xn_next, win_ref, lb_row, act_next, c0)
            bi = ci // chunks_per_b
            if single_time_block:
                state_refs = [(s0_ref.at[bi, h], sout_ref.at[bi, h]) for h in range(HGRN_HEADS)]
            else:
                state_refs = [(st_scr.at[bi, h], st_scr.at[bi, h]) for h in range(HGRN_HEADS)]
            _hgrn_chunk(act, yg_scr, state_refs, slice(ci * chunk, (ci + 1) * chunk), hn_row, ones_rows_bf, chunk)
        y = _dot(yg_scr[...].astype(BF16), wout_ref[...])
        y_ref[...] = x_ref[...] + y

    for parity in (0, 1):
        @pl.when(slot == parity)
        def _():
            run(act_scr.at[parity], act_scr.at[1 - parity])

    if not single_time_block:
        @pl.when(t == pl.num_programs(1) - 1)
        def _():
            sout_ref[...] = st_scr[...]


def _hgrn(x, b, t, s0, gmix, win, lb_params, hnorm, wout, layer, *, layer_slot, bb, tb):
    d = x.shape[1]
    chunk = min(HGRN_MAX_CHUNK, t)
    assert t % tb == 0 and tb % chunk == 0 and b % bb == 0 and chunk % SUBLANES == 0
    heads, dk = HGRN_HEADS, HGRN_HEAD_DIM
    vdim = heads * dk
    assert win.shape[2] == 4 * vdim and vdim % PROJ_COLS == 0
    m = bb * tb
    nj = t // tb
    n_steps = (b // bb) * nj

    assert bb == 1 or tb == t
    this_block = lambda i, j: (i * nj + j, 0)
    next_block = lambda i, j: (jnp.minimum(i * nj + j + 1, n_steps - 1), 0)

    assert wout.shape[1:] == (vdim, d) and d % PROJ_COLS == 0 and win.shape[1] == vdim == d
    scratch = [pltpu.VMEM((2, m, 5 * vdim), F32), pltpu.VMEM((m, vdim), F32),
               pltpu.VMEM(win.shape[1:], BF16), pltpu.VMEM(wout.shape[1:], BF16),
               pltpu.VMEM((2, d, PROJ_COLS), F32), pltpu.SemaphoreType.DMA((2,))]
    if nj > 1:
        scratch.append(pltpu.VMEM((bb, heads, dk, dk), F32))
    return pl.pallas_call(
        functools.partial(_hgrn_kernel, bb=bb, tb=tb, chunk=chunk, layer_slot=layer_slot),
        grid=(b // bb, nj),
        in_specs=[
            pl.BlockSpec((m, d), this_block),
            pl.BlockSpec((m, d), next_block),
            pl.BlockSpec((None, bb, heads, dk, dk), lambda i, j: (layer_slot, i, 0, 0, 0)),
            _layer_spec(gmix, layer, 2),
            pl.BlockSpec(memory_space=pl.ANY),
            pl.BlockSpec(lb_params.shape, lambda i, j: (0, 0)),
            _layer_spec(hnorm, layer_slot, 2),
            pl.BlockSpec(memory_space=pl.ANY),
        ],
        out_specs=[
            pl.BlockSpec((m, d), this_block),
            pl.BlockSpec((bb, heads, dk, dk), lambda i, j: (i, 0, 0, 0)),
        ],
        out_shape=[
            jax.ShapeDtypeStruct((b * t, d), F32),
            jax.ShapeDtypeStruct((b, heads, dk, dk), F32),
        ],
        scratch_shapes=scratch,
        compiler_params=pltpu.CompilerParams(
            dimension_semantics=("arbitrary", "arbitrary"), vmem_limit_bytes=VMEM_LIMIT_BYTES),
        name="hgrn_mixer",
    )(x, x, s0, gmix, win, lb_params, hnorm, wout)


def _conv_kernel(x_ref, buf_ref, gmix_ref, win_ref, cw_ref, wout_ref, y_ref, bufo_ref, ext_scr, *, bb, tb):
    t = pl.program_id(1)
    d = x_ref.shape[-1]
    m = bb * tb
    halo = SUBLANES
    tail = CONV_W - 1

    @pl.when(t == 0)
    def _():
        ext_scr[:, halo - tail:halo, :] = buf_ref[...]

    x = x_ref[...].reshape(m, d)
    xn = _rms(x, gmix_ref[...]).astype(BF16)
    b_gate = _dot(xn, win_ref[:, 0:d].astype(BF16))
    c_gate = _dot(xn, win_ref[:, d:2 * d].astype(BF16))
    v = _dot(xn, win_ref[:, 2 * d:3 * d].astype(BF16))
    ext_scr[:, halo:halo + tb, :] = (c_gate * v).reshape(bb, tb, d)

    cw = cw_ref[...]
    conv = None
    for tap in range(CONV_W):
        start = halo - tail + tap
        term = cw[tap:tap + 1] * ext_scr[:, start:start + tb, :]
        conv = term if conv is None else conv + term
    y = _dot((b_gate * conv.reshape(m, d)).astype(BF16), wout_ref[...].astype(BF16))
    y_ref[...] = (x + y).reshape(bb, tb, d)

    new_tail = ext_scr[:, halo + tb - tail:halo + tb, :]
    ext_scr[:, halo - tail:halo, :] = new_tail

    @pl.when(t == pl.num_programs(1) - 1)
    def _():
        bufo_ref[...] = new_tail


def _conv(x, buf, gmix, win, cw, wout, layer, *, layer_slot, bb, tb):
    b, t, d = x.shape
    tail = CONV_W - 1
    assert t % tb == 0 and b % bb == 0 and tb >= tail
    return pl.pallas_call(
        functools.partial(_conv_kernel, bb=bb, tb=tb),
        grid=(b // bb, t // tb),
        in_specs=[
            pl.BlockSpec((bb, tb, d), lambda i, j: (i, j, 0)),
            pl.BlockSpec((None, bb, tail, d), lambda i, j: (layer_slot, i, 0, 0)),
            _layer_spec(gmix, layer, 2),
            _layer_spec(win, layer_slot, 2),
            _layer_spec(cw, layer_slot, 2),
            _layer_spec(wout, layer_slot, 2),
        ],
        out_specs=[
            pl.BlockSpec((bb, tb, d), lambda i, j: (i, j, 0)),
            pl.BlockSpec((bb, tail, d), lambda i, j: (i, 0, 0)),
        ],
        out_shape=[
            jax.ShapeDtypeStruct((b, t, d), F32),
            jax.ShapeDtypeStruct((b, tail, d), F32),
        ],
        scratch_shapes=[pltpu.VMEM((bb, tb + SUBLANES, d), F32)],
        compiler_params=pltpu.CompilerParams(
            dimension_semantics=("arbitrary", "arbitrary"), vmem_limit_bytes=VMEM_LIMIT_BYTES),
        name="conv_mixer",
    )(x, buf, gmix, win, cw, wout)


def kernel(x_prompt, x_sample, state_hgrn, state_conv, norm_ffn1, w_ffn1_up, w_ffn1_down, norm_mix, norm_ffn2, w_ffn2_up, w_ffn2_down, w_hgrn_in, hgrn_lower_bounds, hgrn_norm, w_hgrn_out, w_conv_in, conv_w, w_conv_out, norm_final):
    g_ffn1, g_mix, g_ffn2, g_hgrn = _rows(norm_ffn1), _rows(norm_mix), _rows(norm_ffn2), _rows(hgrn_norm)
    w1_up, w1_dn, w2_up, w2_dn = w_ffn1_up, w_ffn1_down, w_ffn2_up, w_ffn2_down
    wc_in, wc_out = w_conv_in, w_conv_out
    wh_in, wh_out = w_hgrn_in, w_hgrn_out
    lb_params = hgrn_lower_bounds.astype(F32)

    (bp, tp, d), (bs, ts, _) = x_prompt.shape, x_sample.shape
    n_hgrn, _, heads, dk, dv = state_hgrn.shape
    n_conv, _, tail, _ = state_conv.shape
    depth = norm_ffn1.shape[0]
    zero_hgrn = jnp.zeros((n_hgrn, bp, heads, dk, dv), x_prompt.dtype)
    zero_conv = jnp.zeros((n_conv, bp, tail, d), x_prompt.dtype)

    xp, xs = x_prompt.reshape(bp * tp, d), x_sample.reshape(bs * ts, d)
    hgrn_p, hgrn_s, conv_p, conv_s = [], [], [], []
    for layer in range(depth):
        xp, xs = _ffn(xp, xs, g_ffn1, w1_up, w1_dn, norm_final, layer, final_norm=False, tm=FFN_ROWS)
        j = layer // 2
        if layer % 2 == 0:
            xp, s = _hgrn(xp, bp, tp, zero_hgrn, g_mix, wh_in, lb_params, g_hgrn, wh_out, layer,
                          layer_slot=j, bb=1, tb=HGRN_PROMPT_ROWS)
            hgrn_p.append(s)
            xs, s = _hgrn(xs, bs, ts, state_hgrn, g_mix, wh_in, lb_params, g_hgrn, wh_out, layer,
                          layer_slot=j, bb=HGRN_SAMPLE_SEQS, tb=ts)
            hgrn_s.append(s)
        else:
            x3, s = _conv(xp.reshape(bp, tp, d), zero_conv, g_mix, wc_in, conv_w, wc_out, layer,
                          layer_slot=j, bb=1, tb=CONV_PROMPT_ROWS)
            xp = x3.reshape(bp * tp, d)
            conv_p.append(s)
            x3, s = _conv(xs.reshape(bs, ts, d), state_conv, g_mix, wc_in, conv_w, wc_out, layer,
                          layer_slot=j, bb=CONV_SAMPLE_SEQS, tb=ts)
            xs = x3.reshape(bs * ts, d)
            conv_s.append(s)
        xp, xs = _ffn(xp, xs, g_ffn2, w2_up, w2_dn, norm_final, layer,
                      final_norm=(layer == depth - 1), tm=FFN_ROWS)
    return (xp.reshape(bp, tp, d), xs.reshape(bs, ts, d), jnp.stack(hgrn_p), jnp.stack(hgrn_s),
            jnp.stack(conv_p), jnp.stack(conv_s))
```

```python
import functools

import jax
import jax.numpy as jnp
from jax import lax
from jax.experimental import pallas as pl
from jax.experimental.pallas import tpu as pltpu

F32 = jnp.float32
BF16 = jnp.bfloat16

NORM_EPS = 1e-6
FFN_RES = 0.5
HGRN_HEADS = 8
HGRN_HEAD_DIM = 128
HGRN_MAX_CHUNK = 32
CONV_W = 3
SUBLANES = 8
BF16_ROWS = 16
LANES = 128
FFN_CHUNK = 256
PROJ_COLS = 256
VMEM_LIMIT_BYTES = 58 * 1024 * 1024
FFN_ROWS = 512
HGRN_PROMPT_ROWS = 512
HGRN_SAMPLE_SEQS = 16
CONV_PROMPT_ROWS = 512
CONV_SAMPLE_SEQS = 16


def _rms(x, gain):
    r = lax.rsqrt(jnp.mean(x * x, axis=-1, keepdims=True) + NORM_EPS)
    return (x * r) * gain


def _dot(a, b):
    return jnp.dot(a, b, preferred_element_type=F32)


def _dot_nt(a, b):
    return lax.dot_general(a, b, (((1,), (1,)), ((), ())), preferred_element_type=F32)


def _dot_tn(a, b):
    return lax.dot_general(a, b, (((0,), (0,)), ((), ())), preferred_element_type=F32)


def _layer_spec(w, layer, n_grid):
    zeros = (0,) * (w.ndim - 1)
    index_map = (lambda i: (layer,) + zeros) if n_grid == 1 else (lambda i, j: (layer,) + zeros)
    return pl.BlockSpec((None,) + w.shape[1:], index_map, pipeline_mode=pl.Buffered(1))


def _rows(p):
    return p.reshape(p.shape[0], 1, p.shape[1])


def _ffn_kernel(xa_ref, xb_ref, g_ref, wup_hbm, wdn_hbm, gf_ref, oa_ref, ob_ref,
                wup_bf, wdn_bf, stage_up, stage_dn, sems, *, layer, n_a, final_norm):
    ff = wdn_bf.shape[0]
    chunk_starts = list(range(0, ff, FFN_CHUNK))

    def chunk_copies(ci):
        c0, slot = chunk_starts[ci], ci % 2
        return (
            pltpu.make_async_copy(wup_hbm.at[layer, :, pl.ds(c0, FFN_CHUNK)], stage_up.at[slot, 0], sems.at[slot, 0]),
            pltpu.make_async_copy(wup_hbm.at[layer, :, pl.ds(ff + c0, FFN_CHUNK)], stage_up.at[slot, 1],
                                  sems.at[slot, 1]),
            pltpu.make_async_copy(wdn_hbm.at[layer, pl.ds(c0, FFN_CHUNK), :], stage_dn.at[slot], sems.at[slot, 2]),
        )

    def tile(x_ref, o_ref, stream_weights):
        x = x_ref[...]
        if stream_weights:
            for copy in chunk_copies(0):
                copy.start()
        xn = _rms(x, g_ref[...]).astype(BF16)
        acc = jnp.zeros_like(x)
        for ci, c0 in enumerate(chunk_starts):
            if stream_weights:
                if ci + 1 < len(chunk_starts):
                    for copy in chunk_copies(ci + 1):
                        copy.start()
                for copy in chunk_copies(ci):
                    copy.wait()
                slot = ci % 2
                wup_bf[:, c0:c0 + FFN_CHUNK] = stage_up[slot, 0].astype(BF16)
                wup_bf[:, ff + c0:ff + c0 + FFN_CHUNK] = stage_up[slot, 1].astype(BF16)
                wdn_bf[c0:c0 + FFN_CHUNK, :] = stage_dn[slot].astype(BF16)
            gate = _dot(xn, wup_bf[:, c0:c0 + FFN_CHUNK])
            up = _dot(xn, wup_bf[:, ff + c0:ff + c0 + FFN_CHUNK])
            act = (gate * jax.nn.sigmoid(gate) * up).astype(BF16)
            acc = acc + _dot(act, wdn_bf[c0:c0 + FFN_CHUNK, :])
        out = x + FFN_RES * acc
        if final_norm:
            out = _rms(out, gf_ref[...])
        o_ref[...] = out

    step = pl.program_id(0)
    pl.when(step == 0)(lambda: tile(xa_ref, oa_ref, True))
    pl.when((step > 0) & (step < n_a))(lambda: tile(xa_ref, oa_ref, False))
    pl.when(step >= n_a)(lambda: tile(xb_ref, ob_ref, False))


def _ffn(xa, xb, gains, wup, wdn, gain_final, layer, *, final_norm, tm):
    (na, d), nb = xa.shape, xb.shape[0]
    ff = wdn.shape[1]
    assert na % tm == 0 and nb % tm == 0 and ff % FFN_CHUNK == 0 and wup.shape[2] == 2 * ff
    n_a, n_b = na // tm, nb // tm
    a_block = lambda i: (jnp.minimum(i, n_a - 1), 0)
    b_block = lambda i: (jnp.maximum(i - n_a, 0), 0)
    return pl.pallas_call(
        functools.partial(_ffn_kernel, layer=layer, n_a=n_a, final_norm=final_norm),
        grid=(n_a + n_b,),
        in_specs=[
            pl.BlockSpec((tm, d), a_block),
            pl.BlockSpec((tm, d), b_block),
            _layer_spec(gains, layer, 1),
            pl.BlockSpec(memory_space=pl.ANY),
            pl.BlockSpec(memory_space=pl.ANY),
            pl.BlockSpec((1, d), lambda i: (0, 0)),
        ],
        out_specs=[pl.BlockSpec((tm, d), a_block), pl.BlockSpec((tm, d), b_block)],
        out_shape=[jax.ShapeDtypeStruct((na, d), F32), jax.ShapeDtypeStruct((nb, d), F32)],
        scratch_shapes=[
            pltpu.VMEM((d, 2 * ff), BF16),
            pltpu.VMEM((ff, d), BF16),
            pltpu.VMEM((2, 2, d, FFN_CHUNK), F32),
            pltpu.VMEM((2, FFN_CHUNK, d), F32),
            pltpu.SemaphoreType.DMA((2, 3)),
        ],
        compiler_params=pltpu.CompilerParams(
            dimension_semantics=("arbitrary",), vmem_limit_bytes=VMEM_LIMIT_BYTES),
        name="ffn_final" if final_norm else "ffn",
    )(xa, xb, gains, wup, wdn, gain_final.reshape(1, d))


def _hgrn_near_pairs(act_scr, o_scr, rows, h):
    dk = HGRN_HEAD_DIM
    vdim = HGRN_HEADS * dk
    lo = h * dk
    q = act_scr[rows, lo:lo + dk]
    fg = act_scr[rows, vdim + lo:vdim + lo + dk]
    v = act_scr[rows, 3 * vdim + lo:3 * vdim + lo + dk]
    n_rows = q.shape[0]
    shape3 = (n_rows // SUBLANES, SUBLANES, dk)
    q3, fg3, v3 = q.reshape(shape3), fg.reshape(shape3), v.reshape(shape3)
    k3 = 1.0 - fg3
    ridx = lax.broadcasted_iota(jnp.int32, shape3, 1)
    fm = jnp.where(ridx == 0, 0.0, fg3)
    g, v_shift = k3, v3
    o3 = jnp.sum(q3 * g, axis=-1, keepdims=True) * v_shift
    for _ in range(1, SUBLANES):
        g = fm * pltpu.roll(g, 1, 1)
        v_shift = pltpu.roll(v_shift, 1, 1)
        o3 = o3 + jnp.sum(q3 * g, axis=-1, keepdims=True) * v_shift
    o_scr[rows, lo:lo + dk] = o3.reshape(n_rows, dk)


def _hgrn_operands(act_scr, rows, h, chunk):
    dk = HGRN_HEAD_DIM
    vdim = HGRN_HEADS * dk
    nb = chunk // SUBLANES
    lo = h * dk
    q = act_scr[rows, lo:lo + dk]
    fg = act_scr[rows, vdim + lo:vdim + lo + dk]
    lf = act_scr[rows, 2 * vdim + lo:2 * vdim + lo + dk]
    v = act_scr[rows, 3 * vdim + lo:3 * vdim + lo + dk]
    k = 1.0 - fg

    shape3 = (nb, SUBLANES, dk)
    q3, k3, v3 = q.reshape(shape3), k.reshape(shape3), v.reshape(shape3)
    ridx = lax.broadcasted_iota(jnp.int32, shape3, 1)

    w3 = lf.reshape(shape3)
    for s in (1, 2, 4):
        w3 = w3 + jnp.where(ridx >= s, pltpu.roll(w3, s, 1), 0.0)
    tot = w3[:, SUBLANES - 1:SUBLANES, :]
    offs = [jnp.zeros((1, 1, dk), F32)]
    for j in range(nb):
        offs.append(offs[-1] + tot[j:j + 1])

    q_blk = q3 * jnp.exp2(w3)
    k_blk = k3 * jnp.exp2(tot - w3)
    if nb > 1:
        q_in = q_blk * jnp.exp2(jnp.concatenate(offs[:nb], axis=0))
        k_end = k_blk * jnp.exp2(jnp.concatenate([offs[nb] - offs[j + 1] for j in range(nb)], axis=0))
    else:
        q_in, k_end = q_blk, k_blk

    k_st = v_st = far_mask = None
    if nb > 1:
        k_rows, v_rows, starts = [], [], []
        n_rows = 0
        for i in range(1, nb):
            starts.append(n_rows)
            for j in range(i):
                k_rows.append(k_blk[j] * jnp.exp2(offs[i][0] - offs[j + 1][0]))
                v_rows.append(v3[j])
                n_rows += SUBLANES
        pad = -n_rows % BF16_ROWS
        if pad:
            k_rows.append(jnp.zeros((pad, dk), F32))
            v_rows.append(jnp.zeros((pad, dk), F32))
        k_st = jnp.concatenate(k_rows, axis=0).astype(BF16)
        v_st = jnp.concatenate(v_rows, axis=0).astype(BF16)
        r_tot = n_rows + pad
        row_blk = lax.broadcasted_iota(jnp.int32, (chunk, r_tot), 0) // SUBLANES
        col = lax.broadcasted_iota(jnp.int32, (chunk, r_tot), 1)
        col_blk = jnp.where(col < n_rows, 1, 0)
        for s0 in starts[1:]:
            col_blk = col_blk + jnp.where(col >= s0, 1, 0)
        far_mask = row_blk == col_blk

    e_tot = jnp.exp2(offs[nb][0])
    e_hi = e_tot.astype(BF16).astype(F32)
    e_mid = (e_tot - e_hi).astype(BF16).astype(F32)
    e_lo = e_tot - e_hi - e_mid
    prow = lax.broadcasted_iota(jnp.int32, (BF16_ROWS, dk), 0)
    pieces = jnp.where(prow == 0, e_hi, jnp.where(prow == 1, e_mid, jnp.where(prow == 2, e_lo, 0.0)))
    lhs_state = jnp.concatenate([k_end.reshape(chunk, dk), pieces], axis=0).astype(BF16)
    v_pad = jnp.concatenate([v, jnp.zeros((BF16_ROWS, dk), F32)], axis=0).astype(BF16)
    return (q_in.reshape(chunk, dk).astype(BF16), q_blk.reshape(chunk, dk).astype(BF16),
            k_st, v_st, far_mask, lhs_state, v_pad)


def _hgrn_chunk(act_scr, yg_scr, state_refs, rows, hn_row, ones_rows_bf, chunk):
    dk = HGRN_HEAD_DIM
    vdim = HGRN_HEADS * dk

    def finish(h, o, p, v_st):
        o = o + yg_scr[rows, h * dk:(h + 1) * dk]
        if p is not None:
            o = o + _dot(p, v_st)
        ms = jnp.mean(o * o, axis=-1, keepdims=True)
        gs = act_scr[rows, 4 * vdim + h * dk:4 * vdim + (h + 1) * dk]
        yg_scr[rows, h * dk:(h + 1) * dk] = ((o * lax.rsqrt(ms + NORM_EPS)) * hn_row[:, h * dk:(h + 1) * dk]) * gs

    all_heads = list(range(HGRN_HEADS))
    if chunk > SUBLANES:
        first = _hgrn_first_matmuls(act_scr, state_refs, rows, ones_rows_bf, chunk, all_heads)
        for h in all_heads:
            _hgrn_near_pairs(act_scr, yg_scr, rows, h)
    else:
        first = {}
        for h in all_heads:
            _hgrn_near_pairs(act_scr, yg_scr, rows, h)
            first.update(_hgrn_first_matmuls(act_scr, state_refs, rows, ones_rows_bf, chunk, [h]))
    for h in all_heads:
        finish(h, *first[h])


def _hgrn_first_matmuls(act_scr, state_refs, rows, ones_rows_bf, chunk, heads):
    dk = HGRN_HEAD_DIM
    ops = {h: _hgrn_operands(act_scr, rows, h, chunk) for h in heads}
    st = {h: state_refs[h][0][...] for h in heads}
    o = {h: _dot(ops[h][0], st[h].astype(BF16)) for h in heads}
    p = {h: None for h in heads}
    if chunk > SUBLANES:
        p = {h: jnp.where(ops[h][4], _dot_nt(ops[h][1], ops[h][2]), 0.0).astype(BF16) for h in heads}
    for h in heads:
        res = _dot_tn(ops[h][5], jnp.concatenate([ops[h][6], ones_rows_bf], axis=1))
        state_refs[h][1][...] = res[:, dk:] * st[h] + res[:, :dk]
    return {h: (o[h], p[h], ops[h][3]) for h in heads}


def _hgrn_normed(x_ref, gmix_ref):
    return _rms(x_ref[...], gmix_ref[...]).astype(BF16)


def _hgrn_project_cols(xn, win_ref, lb_row, act, c0):
    vdim = HGRN_HEADS * HGRN_HEAD_DIM
    blk = _dot(xn, win_ref[:, c0:c0 + PROJ_COLS])
    sect, s0 = divmod(c0, vdim)
    if sect == 0:
        act[:, s0:s0 + PROJ_COLS] = blk * jax.nn.sigmoid(blk)
    elif sect == 1:
        lb = lb_row[:, s0:s0 + PROJ_COLS]
        fgate = lb + (1.0 - lb) * jax.nn.sigmoid(blk)
        act[:, vdim + s0:vdim + s0 + PROJ_COLS] = fgate
        act[:, 2 * vdim + s0:2 * vdim + s0 + PROJ_COLS] = jnp.log2(fgate)
    elif sect == 2:
        act[:, 3 * vdim + s0:3 * vdim + s0 + PROJ_COLS] = blk
    else:
        act[:, 4 * vdim + s0:4 * vdim + s0 + PROJ_COLS] = jax.nn.sigmoid(blk)


def _hgrn_kernel(x_ref, xnext_ref, s0_ref, gmix_ref, win_ref, lbp_ref, hn_ref, wout_ref, y_ref, sout_ref,
                 act_scr, yg_scr, st_scr=None, *, bb, tb, chunk, layer_slot):
    t = pl.program_id(1)
    step = pl.program_id(0) * pl.num_programs(1) + t
    slot = lax.rem(step, 2)
    d = x_ref.shape[-1]
    m = bb * tb
    dk = HGRN_HEAD_DIM
    single_time_block = st_scr is None

    if not single_time_block:
        @pl.when(t == 0)
        def _():
            st_scr[...] = s0_ref[...]

    lbp = lbp_ref[...]
    e = jnp.exp(lbp - jnp.max(lbp, axis=0, keepdims=True))
    sm = e / jnp.sum(e, axis=0, keepdims=True)
    lb_row = sm[0:1]
    for s in range(1, layer_slot + 1):
        lb_row = lb_row + sm[s:s + 1]

    proj_cols = list(range(0, 4 * HGRN_HEADS * dk, PROJ_COLS))

    @pl.when(step == 0)
    def _():
        xn0 = _hgrn_normed(x_ref, gmix_ref)
        for c0 in proj_cols:
            _hgrn_project_cols(xn0, win_ref, lb_row, act_scr.at[0], c0)

    def run(act, act_next):
        xn_next = _hgrn_normed(xnext_ref, gmix_ref)
        hn_row = hn_ref[...]
        orow = lax.broadcasted_iota(jnp.int32, (chunk + BF16_ROWS, dk), 0)
        ones_rows_bf = jnp.where((orow >= chunk) & (orow < chunk + 3), 1.0, 0.0).astype(BF16)
        chunks_per_b = tb // chunk
        n_chunks = m // chunk
        for ci in range(n_chunks):
            for c0 in proj_cols[ci * len(proj_cols) // n_chunks:(ci + 1) * len(proj_cols) // n_chunks]:
                _hgrn_project_cols(xn_next, win_ref, lb_row, act_next, c0)
            bi = ci // chunks_per_b
            if single_time_block:
                state_refs = [(s0_ref.at[bi, h], sout_ref.at[bi, h]) for h in range(HGRN_HEADS)]
            else:
                state_refs = [(st_scr.at[bi, h], st_scr.at[bi, h]) for h in range(HGRN_HEADS)]
            _hgrn_chunk(act, yg_scr, state_refs, slice(ci * chunk, (ci + 1) * chunk), hn_row, ones_rows_bf, chunk)
        y = _dot(yg_scr[...].astype(BF16), wout_ref[...])
        y_ref[...] = x_ref[...] + y

    for parity in (0, 1):
        @pl.when(slot == parity)
        def _():
            run(act_scr.at[parity], act_scr.at[1 - parity])

    if not single_time_block:
        @pl.when(t == pl.num_programs(1) - 1)
        def _():
            sout_ref[...] = st_scr[...]


def _hgrn(x, b, t, s0, gmix, win, lb_params, hnorm, wout, layer, *, layer_slot, bb, tb):
    d = x.shape[1]
    chunk = min(HGRN_MAX_CHUNK, t)
    assert t % tb == 0 and tb % chunk == 0 and b % bb == 0 and chunk % SUBLANES == 0
    heads, dk = HGRN_HEADS, HGRN_HEAD_DIM
    vdim = heads * dk
    assert win.shape[2] == 4 * vdim and vdim % PROJ_COLS == 0
    m = bb * tb
    nj = t // tb
    n_steps = (b // bb) * nj

    assert bb == 1 or tb == t
    this_block = lambda i, j: (i * nj + j, 0)
    next_block = lambda i, j: (jnp.minimum(i * nj + j + 1, n_steps - 1), 0)

    scratch = [pltpu.VMEM((2, m, 5 * vdim), F32), pltpu.VMEM((m, vdim), F32)]
    if nj > 1:
        scratch.append(pltpu.VMEM((bb, heads, dk, dk), F32))
    return pl.pallas_call(
        functools.partial(_hgrn_kernel, bb=bb, tb=tb, chunk=chunk, layer_slot=layer_slot),
        grid=(b // bb, nj),
        in_specs=[
            pl.BlockSpec((m, d), this_block),
            pl.BlockSpec((m, d), next_block),
            pl.BlockSpec((None, bb, heads, dk, dk), lambda i, j: (layer_slot, i, 0, 0, 0)),
            _layer_spec(gmix, layer, 2),
            _layer_spec(win, layer_slot, 2),
            pl.BlockSpec(lb_params.shape, lambda i, j: (0, 0)),
            _layer_spec(hnorm, layer_slot, 2),
            _layer_spec(wout, layer_slot, 2),
        ],
        out_specs=[
            pl.BlockSpec((m, d), this_block),
            pl.BlockSpec((bb, heads, dk, dk), lambda i, j: (i, 0, 0, 0)),
        ],
        out_shape=[
            jax.ShapeDtypeStruct((b * t, d), F32),
            jax.ShapeDtypeStruct((b, heads, dk, dk), F32),
        ],
        scratch_shapes=scratch,
        compiler_params=pltpu.CompilerParams(
            dimension_semantics=("arbitrary", "arbitrary"), vmem_limit_bytes=VMEM_LIMIT_BYTES),
        name="hgrn_mixer",
    )(x, x, s0, gmix, win, lb_params, hnorm, wout)


def _conv_kernel(x_ref, buf_ref, gmix_ref, win_ref, cw_ref, wout_ref, y_ref, bufo_ref, ext_scr, *, bb, tb):
    t = pl.program_id(1)
    d = x_ref.shape[-1]
    m = bb * tb
    halo = SUBLANES
    tail = CONV_W - 1

    @pl.when(t == 0)
    def _():
        ext_scr[:, halo - tail:halo, :] = buf_ref[...]

    x = x_ref[...].reshape(m, d)
    xn = _rms(x, gmix_ref[...]).astype(BF16)
    b_gate = _dot(xn, win_ref[:, 0:d].astype(BF16))
    c_gate = _dot(xn, win_ref[:, d:2 * d].astype(BF16))
    v = _dot(xn, win_ref[:, 2 * d:3 * d].astype(BF16))
    ext_scr[:, halo:halo + tb, :] = (c_gate * v).reshape(bb, tb, d)

    cw = cw_ref[...]
    conv = None
    for tap in range(CONV_W):
        start = halo - tail + tap
        term = cw[tap:tap + 1] * ext_scr[:, start:start + tb, :]
        conv = term if conv is None else conv + term
    y = _dot((b_gate * conv.reshape(m, d)).astype(BF16), wout_ref[...].astype(BF16))
    y_ref[...] = (x + y).reshape(bb, tb, d)

    new_tail = ext_scr[:, halo + tb - tail:halo + tb, :]
    ext_scr[:, halo - tail:halo, :] = new_tail

    @pl.when(t == pl.num_programs(1) - 1)
    def _():
        bufo_ref[...] = new_tail


def _conv(x, buf, gmix, win, cw, wout, layer, *, layer_slot, bb, tb):
    b, t, d = x.shape
    tail = CONV_W - 1
    assert t % tb == 0 and b % bb == 0 and tb >= tail
    return pl.pallas_call(
        functools.partial(_conv_kernel, bb=bb, tb=tb),
        grid=(b // bb, t // tb),
        in_specs=[
            pl.BlockSpec((bb, tb, d), lambda i, j: (i, j, 0)),
            pl.BlockSpec((None, bb, tail, d), lambda i, j: (layer_slot, i, 0, 0)),
            _layer_spec(gmix, layer, 2),
            _layer_spec(win, layer_slot, 2),
            _layer_spec(cw, layer_slot, 2),
            _layer_spec(wout, layer_slot, 2),
        ],
        out_specs=[
            pl.BlockSpec((bb, tb, d), lambda i, j: (i, j, 0)),
            pl.BlockSpec((bb, tail, d), lambda i, j: (i, 0, 0)),
        ],
        out_shape=[
            jax.ShapeDtypeStruct((b, t, d), F32),
            jax.ShapeDtypeStruct((b, tail, d), F32),
        ],
        scratch_shapes=[pltpu.VMEM((bb, tb + SUBLANES, d), F32)],
        compiler_params=pltpu.CompilerParams(
            dimension_semantics=("arbitrary", "arbitrary"), vmem_limit_bytes=VMEM_LIMIT_BYTES),
        name="conv_mixer",
    )(x, buf, gmix, win, cw, wout)


def kernel(x_prompt, x_sample, state_hgrn, state_conv, norm_ffn1, w_ffn1_up, w_ffn1_down, norm_mix, norm_ffn2, w_ffn2_up, w_ffn2_down, w_hgrn_in, hgrn_lower_bounds, hgrn_norm, w_hgrn_out, w_conv_in, conv_w, w_conv_out, norm_final):
    g_ffn1, g_mix, g_ffn2, g_hgrn = _rows(norm_ffn1), _rows(norm_mix), _rows(norm_ffn2), _rows(hgrn_norm)
    w1_up, w1_dn, w2_up, w2_dn = w_ffn1_up, w_ffn1_down, w_ffn2_up, w_ffn2_down
    wc_in, wc_out = w_conv_in, w_conv_out
    wh_in, wh_out = w_hgrn_in.astype(BF16), w_hgrn_out.astype(BF16)
    lb_params = hgrn_lower_bounds.astype(F32)

    (bp, tp, d), (bs, ts, _) = x_prompt.shape, x_sample.shape
    n_hgrn, _, heads, dk, dv = state_hgrn.shape
    n_conv, _, tail, _ = state_conv.shape
    depth = norm_ffn1.shape[0]
    zero_hgrn = jnp.zeros((n_hgrn, bp, heads, dk, dv), x_prompt.dtype)
    zero_conv = jnp.zeros((n_conv, bp, tail, d), x_prompt.dtype)

    xp, xs = x_prompt.reshape(bp * tp, d), x_sample.reshape(bs * ts, d)
    hgrn_p, hgrn_s, conv_p, conv_s = [], [], [], []
    for layer in range(depth):
        xp, xs = _ffn(xp, xs, g_ffn1, w1_up, w1_dn, norm_final, layer, final_norm=False, tm=FFN_ROWS)
        j = layer // 2
        if layer % 2 == 0:
            xp, s = _hgrn(xp, bp, tp, zero_hgrn, g_mix, wh_in, lb_params, g_hgrn, wh_out, layer,
                          layer_slot=j, bb=1, tb=HGRN_PROMPT_ROWS)
            hgrn_p.append(s)
            xs, s = _hgrn(xs, bs, ts, state_hgrn, g_mix, wh_in, lb_params, g_hgrn, wh_out, layer,
                          layer_slot=j, bb=HGRN_SAMPLE_SEQS, tb=ts)
            hgrn_s.append(s)
        else:
            x3, s = _conv(xp.reshape(bp, tp, d), zero_conv, g_mix, wc_in, conv_w, wc_out, layer,
                          layer_slot=j, bb=1, tb=CONV_PROMPT_ROWS)
            xp = x3.reshape(bp * tp, d)
            conv_p.append(s)
            x3, s = _conv(xs.reshape(bs, ts, d), state_conv, g_mix, wc_in, conv_w, wc_out, layer,
                          layer_slot=j, bb=CONV_SAMPLE_SEQS, tb=ts)
            xs = x3.reshape(bs * ts, d)
            conv_s.append(s)
        xp, xs = _ffn(xp, xs, g_ffn2, w2_up, w2_dn, norm_final, layer,
                      final_norm=(layer == depth - 1), tm=FFN_ROWS)
    return (xp.reshape(bp, tp, d), xs.reshape(bs, ts, d), jnp.stack(hgrn_p), jnp.stack(hgrn_s),
            jnp.stack(conv_p), jnp.stack(conv_s))
```
